```python
import jax, jax.numpy as jnp
from jax import lax
import numpy as np

D_MODEL = 1024
BATCH = 8
SEQ = 2048
DEPTH = 4
DEC_BATCH = 128
DEC_SEQ = 8
PAST_LEN = 16384
PAGE_SIZE = 128

POOL_W = D_MODEL
POOL_GROUPS = 4
POOL_GW = POOL_W // POOL_GROUPS
POOL_WINDOWS = (2, 4, 8, 16)
POOL_PREFIX = 15
LRU_W = D_MODEL
LRU_HEADS = 8
LRU_HD = LRU_W // LRU_HEADS
CONV_W = 4
LRU_C = 8.0
GMLP_W = D_MODEL // 2
GMLP_GROUPS = 4
GMLP_GW = GMLP_W // GMLP_GROUPS
CHUNK = 128
N_BRANCH = 3
IN_W = POOL_W + 2 * LRU_W + 2 * GMLP_W + N_BRANCH * D_MODEL
D_FF = 2816
N_MOD = 9
EPS = 1e-6

kernel_name = "hybrid_pool_lru_gmlp_macaron_decoder_step"


def rmsnorm(x, g):
    xf = x.astype(jnp.float32)
    y = xf * lax.rsqrt(jnp.mean(xf * xf, axis=-1, keepdims=True) + EPS)
    return (y * g.astype(jnp.float32)).astype(x.dtype)


def swiglu(h, wg, wu, wd):
    return (jax.nn.silu(h @ wg) * (h @ wu)) @ wd


def pool_mixer(xp, prefix, start_pos, w_grp, scale):
    B, T, _ = xp.shape
    xcat = jnp.concatenate([prefix.astype(xp.dtype), xp], axis=1)
    cs = jnp.cumsum(xcat.astype(jnp.float32), axis=1)
    cs = jnp.concatenate([jnp.zeros((B, 1, POOL_W), jnp.float32), cs], axis=1)
    pos = start_pos + jnp.arange(T, dtype=jnp.int32)
    hi = cs[:, POOL_PREFIX + 1:POOL_PREFIX + 1 + T]
    means = []
    for g, w in enumerate(POOL_WINDOWS):
        c0 = g * POOL_GW
        lo = cs[:, POOL_PREFIX + 1 - w:POOL_PREFIX + 1 - w + T, c0:c0 + POOL_GW]
        cnt = jnp.minimum(w, pos + 1).astype(jnp.float32)[None, :, None]
        means.append((hi[..., c0:c0 + POOL_GW] - lo) / cnt)
    mean = jnp.stack(means, axis=2)
    d = mean - xp.astype(jnp.float32).reshape(B, T, POOL_GROUPS, POOL_GW)
    out = jnp.einsum('btgc,gcd->btgd', d.astype(xp.dtype), w_grp).reshape(B, T, POOL_W)
    return out * scale, xcat[:, -POOL_PREFIX:]


def causal_conv(xb, prefix, w, b):
    T = xb.shape[1]
    xcat = jnp.concatenate([prefix.astype(xb.dtype), xb], axis=1)
    out = b + sum(xcat[:, k:k + T] * w[k] for k in range(CONV_W))
    return out, xcat[:, -(CONV_W - 1):]


def rg_lru(xc, h0, wr, br, wi, bi, lam):
    B, T, _ = xc.shape
    xh = xc.reshape(B, T, LRU_HEADS, LRU_HD)
    r = jax.nn.sigmoid(jnp.einsum('bthc,hcd->bthd', xh, wr).reshape(B, T, LRU_W) + br)
    i = jax.nn.sigmoid(jnp.einsum('bthc,hcd->bthd', xh, wi).reshape(B, T, LRU_W) + bi)
    log_a = -LRU_C * r.astype(jnp.float32) * jax.nn.softplus(-lam.astype(jnp.float32))
    a = jnp.exp(log_a)
    bx = jnp.sqrt(-jnp.expm1(2.0 * log_a)) * (i * xc).astype(jnp.float32)

    def step(h, ab):
        a_t, b_t = ab
        h = a_t * h + b_t
        return h, h

    hT, hs = lax.scan(step, h0.astype(jnp.float32), (jnp.swapaxes(a, 0, 1), jnp.swapaxes(bx, 0, 1)))
    return jnp.swapaxes(hs, 0, 1).astype(xc.dtype), hT.astype(xc.dtype)


def chunk_gmlp(u, v, g_norm, ws, bs):
    B, T, _ = u.shape
    vn = rmsnorm(v, g_norm)
    Tp = -(-T // CHUNK) * CHUNK
    vpad = jnp.pad(vn, ((0, 0), (0, Tp - T), (0, 0)))
    vb = vpad.reshape(B, Tp // CHUNK, CHUNK, GMLP_GROUPS, GMLP_GW)
    mask = jnp.tril(jnp.ones((CHUNK, CHUNK), dtype=bool))
    wm = jnp.where(mask[None], ws, 0)
    s = jnp.einsum('gts,bnsgc->bntgc', wm, vb) + jnp.transpose(bs)[None, None, :, :, None]
    s = s.reshape(B, Tp, GMLP_W)[:, :T]
    return u * s, vn


def token_mixer(h, pre_pool, pre_conv, h0, start_pos, w_in, pool_w, pool_scale, conv_w, conv_b,
                wr, br, wi, bi, lam, gm_norm, gm_ws, gm_bs, wbo_pool, wbo_lru, wbo_gm, w_out):
    B, T, _ = h.shape
    z = h @ w_in
    idx = np.cumsum([POOL_W, LRU_W, LRU_W, GMLP_W, GMLP_W]).tolist()
    xp, xl, gl, u, v, gates = jnp.split(z, idx, axis=-1)
    y_pool, new_pool = pool_mixer(xp, pre_pool, start_pos, pool_w, pool_scale)
    xc, new_conv = causal_conv(xl, pre_conv, conv_w, conv_b)
    y_lru, hT = rg_lru(xc, h0, wr, br, wi, bi, lam)
    y_lru = y_lru * jax.nn.gelu(gl)
    y_gm, vn = chunk_gmlp(u, v, gm_norm, gm_ws, gm_bs)
    g = jax.nn.sigmoid(gates).reshape(B, T, N_BRANCH, D_MODEL)
    merged = (g[:, :, 0] * (y_pool @ wbo_pool) + g[:, :, 1] * (y_lru @ wbo_lru)
              + g[:, :, 2] * (y_gm @ wbo_gm))
    return merged @ w_out, new_pool, new_conv, hT, vn


def layer(x, c, pre_pool, pre_conv, h0, start_pos, lw):
    (w_ada, b_ada, n1, f1g, f1u, f1d, nm, w_in, pool_w, pool_scale, conv_w, conv_b,
     wr, br, wi, bi, lam, gm_norm, gm_ws, gm_bs, wbo_pool, wbo_lru, wbo_gm, w_out,
     n2, f2g, f2u, f2d) = lw
    mod = jax.nn.silu(c) @ w_ada + b_ada
    sh1, sc1, gt1, sh2, sc2, gt2, sh3, sc3, gt3 = [m[:, None, :] for m in jnp.split(mod, N_MOD, axis=-1)]
    h = rmsnorm(x, n1) * (1 + sc1) + sh1
    x = x + 0.5 * gt1 * swiglu(h, f1g, f1u, f1d)
    h = rmsnorm(x, nm) * (1 + sc2) + sh2
    y, new_pool, new_conv, hT, vn = token_mixer(h, pre_pool, pre_conv, h0, start_pos, w_in, pool_w, pool_scale,
                                                conv_w, conv_b, wr, br, wi, bi, lam, gm_norm, gm_ws, gm_bs,
                                                wbo_pool, wbo_lru, wbo_gm, w_out)
    x = x + gt2 * y
    h = rmsnorm(x, n2) * (1 + sc3) + sh3
    x = x + 0.5 * gt3 * swiglu(h, f2g, f2u, f2d)
    return x, new_pool, new_conv, hT, vn


def setup_inputs(seed: int = 0) -> dict:
    key = jax.random.key(seed)
    ks = jax.random.split(key, 48)
    f32 = jnp.float32
    nrm = lambda k, s, sc: jax.random.normal(k, s, f32) * sc
    D = D_MODEL
    a8 = jax.random.uniform(ks[20], (DEPTH, LRU_W), f32, 0.9, 0.999)
    a_base = a8 ** (1.0 / LRU_C)
    lam = jnp.log(a_base) - jnp.log1p(-a_base)
    return {
        "x_prompt": nrm(ks[0], (BATCH, SEQ, D), 1.0),
        "x_sample": nrm(ks[1], (DEC_BATCH, DEC_SEQ, D), 1.0),
        "c_prompt": nrm(ks[2], (BATCH, D), 1.0),
        "c_sample": nrm(ks[3], (DEC_BATCH, D), 1.0),
        "state_pool": nrm(ks[4], (DEPTH, DEC_BATCH, POOL_PREFIX, POOL_W), 1.0),
        "state_conv": nrm(ks[5], (DEPTH, DEC_BATCH, CONV_W - 1, LRU_W), 1.0),
        "state_lru": nrm(ks[6], (DEPTH, DEC_BATCH, LRU_W), 0.5),
        "w_ada": nrm(ks[7], (DEPTH, D, N_MOD * D), 0.5 * D ** -0.5),
        "b_ada": nrm(ks[8], (DEPTH, N_MOD * D), 0.01),
        "norm_ffn1": 1.0 + nrm(ks[9], (DEPTH, D), 0.01),
        "ffn1_w_gate": nrm(ks[10], (DEPTH, D, D_FF), D ** -0.5),
        "ffn1_w_up": nrm(ks[11], (DEPTH, D, D_FF), D ** -0.5),
        "ffn1_w_down": nrm(ks[12], (DEPTH, D_FF, D), D_FF ** -0.5),
        "norm_mix": 1.0 + nrm(ks[13], (DEPTH, D), 0.01),
        "w_in": nrm(ks[14], (DEPTH, D, IN_W), D ** -0.5),
        "pool_w": nrm(ks[15], (DEPTH, POOL_GROUPS, POOL_GW, POOL_GW), POOL_GW ** -0.5),
        "pool_scale": 1.0 + nrm(ks[16], (DEPTH, POOL_W), 0.1),
        "conv_w": nrm(ks[17], (DEPTH, CONV_W, LRU_W), CONV_W ** -0.5),
        "conv_b": nrm(ks[18], (DEPTH, LRU_W), 0.01),
        "lru_wr": nrm(ks[19], (DEPTH, LRU_HEADS, LRU_HD, LRU_HD), LRU_HD ** -0.5),
        "lru_br": nrm(ks[21], (DEPTH, LRU_W), 0.01),
        "lru_wi": nrm(ks[22], (DEPTH, LRU_HEADS, LRU_HD, LRU_HD), LRU_HD ** -0.5),
        "lru_bi": nrm(ks[23], (DEPTH, LRU_W), 0.01),
        "lru_lambda": lam,
        "gmlp_norm": 1.0 + nrm(ks[24], (DEPTH, GMLP_W), 0.01),
        "gmlp_ws": nrm(ks[25], (DEPTH, GMLP_GROUPS, CHUNK, CHUNK), 0.5 * CHUNK ** -0.5),
        "gmlp_bs": 1.0 + nrm(ks[26], (DEPTH, GMLP_GROUPS, CHUNK), 0.01),
        "wbo_pool": nrm(ks[27], (DEPTH, POOL_W, D), POOL_W ** -0.5),
        "wbo_lru": nrm(ks[28], (DEPTH, LRU_W, D), LRU_W ** -0.5),
        "wbo_gmlp": nrm(ks[29], (DEPTH, GMLP_W, D), GMLP_W ** -0.5),
        "w_out": nrm(ks[30], (DEPTH, D, D), D ** -0.5),
        "norm_ffn2": 1.0 + nrm(ks[31], (DEPTH, D), 0.01),
        "ffn2_w_gate": nrm(ks[32], (DEPTH, D, D_FF), D ** -0.5),
        "ffn2_w_up": nrm(ks[33], (DEPTH, D, D_FF), D ** -0.5),
        "ffn2_w_down": nrm(ks[34], (DEPTH, D_FF, D), D_FF ** -0.5),
        "norm_final": 1.0 + nrm(ks[35], (D,), 0.01),
    }


def reference(x_prompt, x_sample, c_prompt, c_sample, state_pool, state_conv, state_lru,
              w_ada, b_ada, norm_ffn1, ffn1_w_gate, ffn1_w_up, ffn1_w_down, norm_mix, w_in,
              pool_w, pool_scale, conv_w, conv_b, lru_wr, lru_br, lru_wi, lru_bi, lru_lambda,
              gmlp_norm, gmlp_ws, gmlp_bs, wbo_pool, wbo_lru, wbo_gmlp, w_out,
              norm_ffn2, ffn2_w_gate, ffn2_w_up, ffn2_w_down, norm_final):
    xp = x_prompt
    xs = x_sample
    zp_pool = jnp.zeros((BATCH, POOL_PREFIX, POOL_W), x_prompt.dtype)
    zp_conv = jnp.zeros((BATCH, CONV_W - 1, LRU_W), x_prompt.dtype)
    zp_h = jnp.zeros((BATCH, LRU_W), x_prompt.dtype)
    pp, pc, ph, sp, sc, sh, sv = [], [], [], [], [], [], []
    for l in range(DEPTH):
        lw = (w_ada[l], b_ada[l], norm_ffn1[l], ffn1_w_gate[l], ffn1_w_up[l], ffn1_w_down[l],
              norm_mix[l], w_in[l], pool_w[l], pool_scale[l], conv_w[l], conv_b[l],
              lru_wr[l], lru_br[l], lru_wi[l], lru_bi[l], lru_lambda[l],
              gmlp_norm[l], gmlp_ws[l], gmlp_bs[l], wbo_pool[l], wbo_lru[l], wbo_gmlp[l], w_out[l],
              norm_ffn2[l], ffn2_w_gate[l], ffn2_w_up[l], ffn2_w_down[l])
        xp, npool, nconv, nh, _ = layer(xp, c_prompt, zp_pool, zp_conv, zp_h, 0, lw)
        pp.append(npool); pc.append(nconv); ph.append(nh)
        xs, npool, nconv, nh, nv = layer(xs, c_sample, state_pool[l], state_conv[l], state_lru[l], PAST_LEN, lw)
        sp.append(npool); sc.append(nconv); sh.append(nh); sv.append(nv)
    y_prompt = rmsnorm(xp, norm_final)
    y_sample = rmsnorm(xs, norm_final)
    return (y_prompt, y_sample, jnp.stack(pp), jnp.stack(pc), jnp.stack(ph),
            jnp.stack(sp), jnp.stack(sc), jnp.stack(sh), jnp.stack(sv))
```

```python
import functools

import jax
import jax.numpy as jnp
from jax import lax
from jax.experimental import pallas as pl
from jax.experimental.pallas import tpu as pltpu

D = 1024
DEPTH = 4
PAST_LEN = 16384
POOL_WINDOWS = (2, 4, 8, 16)
POOL_GW = 256
POOL_PREFIX = 15
LRU_HEADS = 8
LRU_HD = 128
CONV_W = 4
LRU_C = 8.0
GMLP_W = 512
GMLP_GW = 128
CHUNK = 128
IN_W = 7168
D_FF = 2816
N_MOD = 9
EPS = 1e-6

FF_CHUNK = 256
SUBLANES = 8
PREF = 32
CPREF = 8
VMEM_LIMIT = 56 * 1024 * 1024

F32 = jnp.float32
BF16 = jnp.bfloat16


def _dot(a, b):
    return jnp.dot(a, b, preferred_element_type=F32)


def _norm_mod(x, n, sc, sh):
    ms = jnp.mean(x * x, axis=-1, keepdims=True)
    y = x * lax.rsqrt(ms + EPS) * n
    return y * (1.0 + sc) + sh


def _ada_body(cp_ref, cs_ref, w_ref, b_ref, op_ref, os_ref):
    w = w_ref[...].astype(BF16)
    b = b_ref[...]

    def f(c):
        s = c * jax.nn.sigmoid(c)
        return _dot(s.astype(BF16), w) + b

    op_ref[...] = f(cp_ref[...])
    os_ref[...] = f(cs_ref[...])


def _ada(c_prompt, c_sample, w_ada, b_ada):
    nbp, nbs = c_prompt.shape[0], c_sample.shape[0]
    return pl.pallas_call(
        _ada_body,
        grid=(DEPTH, N_MOD),
        in_specs=[
            pl.BlockSpec((nbp, D), lambda l, j: (0, 0)),
            pl.BlockSpec((nbs, D), lambda l, j: (0, 0)),
            pl.BlockSpec((None, D, D), lambda l, j: (l, 0, j)),
            pl.BlockSpec((None, 1, D), lambda l, j: (l, 0, j)),
        ],
        out_specs=[
            pl.BlockSpec((None, None, nbp, D), lambda l, j: (l, j, 0, 0)),
            pl.BlockSpec((None, None, nbs, D), lambda l, j: (l, j, 0, 0)),
        ],
        out_shape=[
            jax.ShapeDtypeStruct((DEPTH, N_MOD, nbp, D), F32),
            jax.ShapeDtypeStruct((DEPTH, N_MOD, nbs, D), F32),
        ],
        name="ada",
    )(c_prompt, c_sample, w_ada, b_ada.reshape(DEPTH, 1, N_MOD * D))


def _ffn_body(x_ref, sh_ref, sc_ref, gt_ref, n_ref, wg_ref, wu_ref, wd_ref, nf_ref, o_ref, *, final):
    x = x_ref[...]
    nb, t, _ = x.shape
    rows = nb * t
    h = _norm_mod(x, n_ref[...], sc_ref[...], sh_ref[...])
    hb = h.reshape(rows, D).astype(BF16)
    acc = jnp.zeros((rows, D), F32)
    for c in range(D_FF // FF_CHUNK):
        sl = slice(c * FF_CHUNK, (c + 1) * FF_CHUNK)
        g = _dot(hb, wg_ref[:, sl])
        u = _dot(hb, wu_ref[:, sl])
        a = (g * jax.nn.sigmoid(g)) * u
        acc = acc + _dot(a.astype(BF16), wd_ref[sl, :])
    out = x + (0.5 * gt_ref[...]) * acc.reshape(nb, t, D)
    if final:
        ms = jnp.mean(out * out, axis=-1, keepdims=True)
        out = out * lax.rsqrt(ms + EPS) * nf_ref[...]
    o_ref[...] = out


def _mod_spec(l, k, nb):
    return pl.BlockSpec((None, None, nb, 1, D), lambda i, j: (l, k, i, 0, 0))


def _row_spec(l, width):
    return pl.BlockSpec((None, 1, width), lambda i, j: (l, 0, 0))


def _full_spec(l, shape):
    nd = len(shape)
    return pl.BlockSpec((None,) + tuple(shape), lambda i, j: (l,) + (0,) * nd,
                        pipeline_mode=pl.Buffered(1))


def _ffn(x, mods, l, k0, norm, wg, wu, wd, norm_final, *, nb, t, final):
    nbt, tt, _ = x.shape
    grid = (nbt // nb, tt // t)
    xspec = pl.BlockSpec((nb, t, D), lambda i, j: (i, j, 0))
    return pl.pallas_call(
        functools.partial(_ffn_body, final=final),
        grid=grid,
        in_specs=[
            xspec,
            _mod_spec(l, k0, nb), _mod_spec(l, k0 + 1, nb), _mod_spec(l, k0 + 2, nb),
            _row_spec(l, D),
            _full_spec(l, (D, D_FF)), _full_spec(l, (D, D_FF)), _full_spec(l, (D_FF, D)),
            pl.BlockSpec((1, D), lambda i, j: (0, 0)),
        ],
        out_specs=xspec,
        out_shape=jax.ShapeDtypeStruct(x.shape, F32),
        compiler_params=pltpu.CompilerParams(
            dimension_semantics=("arbitrary", "arbitrary"), vmem_limit_bytes=VMEM_LIMIT),
        name="ffn_final" if final else "ffn",
    )(x, mods, mods, mods, norm, wg, wu, wd, norm_final)


def _scan8(a, b):
    row = lax.broadcasted_iota(jnp.int32, a.shape, 0) & (SUBLANES - 1)
    for d in (1, 2, 4):
        m = row >= d
        a_s = jnp.where(m, pltpu.roll(a, d, 0), 1.0)
        b_s = jnp.where(m, pltpu.roll(b, d, 0), 0.0)
        b = a * b_s + b
        a = a * a_s
    return a, b


def _lru_coeffs(xc, wri_ref, br, bi, lam):
    xcb = xc.astype(BF16)
    rs, is_ = [], []
    for hd in range(LRU_HEADS):
        res = _dot(xcb[:, hd * LRU_HD:(hd + 1) * LRU_HD], wri_ref[hd])
        rs.append(res[:, :LRU_HD])
        is_.append(res[:, LRU_HD:])
    r = jax.nn.sigmoid(jnp.concatenate(rs, axis=1) + br)
    i = jax.nn.sigmoid(jnp.concatenate(is_, axis=1) + bi)
    nl = -lam
    softplus = jnp.maximum(nl, 0.0) + jnp.log1p(jnp.exp(-jnp.abs(nl)))
    log_a = (-LRU_C * r) * softplus
    a = jnp.exp(log_a)
    bx = jnp.sqrt(1.0 - a * a) * (i * xc)
    return a, bx


def _pool_project(d_parts, poolw_ref, pscale):
    ys = [_dot(d.astype(BF16), poolw_ref[g]) for g, d in enumerate(d_parts)]
    return jnp.concatenate(ys, axis=1) * pscale


def _rms(v, g):
    ms = jnp.mean(v * v, axis=-1, keepdims=True)
    return v * lax.rsqrt(ms + EPS) * g


def _merge_out(hb, win_ref, y_pool, y_lru, y_gm, wbop_ref, wbol_ref, wbog_ref, wout_ref):
    def gate(k):
        c0 = D + 2 * D + 2 * GMLP_W + k * D
        return jax.nn.sigmoid(_dot(hb, win_ref[:, c0:c0 + D]))

    merged = gate(0) * _dot(y_pool.astype(BF16), wbop_ref[...])
    merged = merged + gate(1) * _dot(y_lru.astype(BF16), wbol_ref[...])
    merged = merged + gate(2) * _dot(y_gm.astype(BF16), wbog_ref[...])
    return _dot(merged.astype(BF16), wout_ref[...])


def _window_sum(p_ref, ta_ref, tb_ref, c0, w, tm):
    src, c_src = p_ref, c0
    bufs = (ta_ref, tb_ref)
    k, level = 1, 0
    while True:
        start = SUBLANES * (level + 1)
        last = 2 * k == w
        if last:
            start = PREF
        n = PREF + tm - start
        cols = slice(c_src, c_src + POOL_GW)
        val = src[pl.ds(start, n), cols] + src[pl.ds(start - k, n), cols]
        if last:
            return val
        dst = bufs[level % 2]
        dst[pl.ds(start, n), :] = val
        src, c_src = dst, 0
        k, level = 2 * k, level + 1


def _mix_prompt_body(x_ref, sh_ref, sc_ref, gt_ref, nm_ref, win_ref, poolw_ref, pscale_ref,
                     convw_ref, convb_ref, wri_ref, br_ref, bi_ref, lam_ref, gnorm_ref, gws_ref,
                     gbs_ref, wbop_ref, wbol_ref, wbog_ref, wout_ref,
                     o_ref, npool_ref, nconv_ref, nh_ref,
                     p_ref, ta_ref, tb_ref, c_ref, a_ref, b_ref, hs_ref, hc_ref, yg_ref, *, tm):
    j = pl.program_id(1)
    nt = pl.num_programs(1)

    @pl.when(j == 0)
    def _():
        p_ref[0:PREF, :] = jnp.zeros((PREF, D), F32)
        ta_ref[0:PREF, :] = jnp.zeros((PREF, POOL_GW), F32)
        tb_ref[0:PREF, :] = jnp.zeros((PREF, POOL_GW), F32)
        c_ref[0:CPREF, :] = jnp.zeros((CPREF, D), F32)
        hc_ref[...] = jnp.zeros((1, D), F32)

    x = x_ref[...]
    h = _norm_mod(x, nm_ref[...], sc_ref[...], sh_ref[...])
    hb = h.reshape(tm, D).astype(BF16)

    xp = _dot(hb, win_ref[:, 0:D])
    p_ref[pl.ds(PREF, tm), :] = xp
    pos = j * tm + lax.broadcasted_iota(jnp.int32, (tm, POOL_GW), 0)
    d_parts = []
    for g, w in enumerate(POOL_WINDOWS):
        s = _window_sum(p_ref, ta_ref, tb_ref, g * POOL_GW, w, tm)
        cnt = jnp.minimum(w, pos + 1).astype(F32)
        d_parts.append(s / cnt - xp[:, g * POOL_GW:(g + 1) * POOL_GW])
    y_pool = _pool_project(d_parts, poolw_ref, pscale_ref[...])

    @pl.when(j == nt - 1)
    def _():
        npool_ref[...] = p_ref[pl.ds(PREF + tm - POOL_PREFIX, POOL_PREFIX), :]

    p_ref[0:PREF, :] = p_ref[pl.ds(tm, PREF), :]

    xl = _dot(hb, win_ref[:, D:2 * D])
    c_ref[pl.ds(CPREF, tm), :] = xl
    cw = convw_ref[...]
    xc = convb_ref[...] + cw[3:4, :] * xl
    for k in range(CONV_W - 1):
        xc = xc + cw[k:k + 1, :] * c_ref[pl.ds(CPREF - (CONV_W - 1) + k, tm), :]

    @pl.when(j == nt - 1)
    def _():
        nconv_ref[...] = c_ref[pl.ds(CPREF + tm - (CONV_W - 1), CONV_W - 1), :]

    c_ref[0:CPREF, :] = c_ref[pl.ds(tm, CPREF), :]

    a, bx = _lru_coeffs(xc, wri_ref, br_ref[...], bi_ref[...], lam_ref[...])
    a, bx = _scan8(a, bx)
    a_ref[...] = a
    b_ref[...] = bx
    hprev = hc_ref[...]
    for grp in range(tm // SUBLANES):
        rows = pl.ds(grp * SUBLANES, SUBLANES)
        hg = a_ref[rows, :] * hprev + b_ref[rows, :]
        hs_ref[rows, :] = hg
        hprev = hg[SUBLANES - 1:SUBLANES, :]
    hc_ref[...] = hprev

    @pl.when(j == nt - 1)
    def _():
        nh_ref[...] = hprev

    gl = _dot(hb, win_ref[:, 2 * D:3 * D])
    y_lru = hs_ref[...] * jax.nn.gelu(gl)

    u = _dot(hb, win_ref[:, 3 * D:3 * D + GMLP_W])
    v = _dot(hb, win_ref[:, 3 * D + GMLP_W:4 * D])
    vnb = _rms(v, gnorm_ref[...]).astype(BF16)
    tri = (lax.broadcasted_iota(jnp.int32, (CHUNK, CHUNK), 0)
           >= lax.broadcasted_iota(jnp.int32, (CHUNK, CHUNK), 1))
    gbs = gbs_ref[...]
    for g in range(GMLP_W // GMLP_GW):
        wm = jnp.where(tri, gws_ref[g], 0.0).astype(BF16)
        cols = slice(g * GMLP_GW, (g + 1) * GMLP_GW)
        for c in range(tm // CHUNK):
            rows = slice(c * CHUNK, (c + 1) * CHUNK)
            s = _dot(wm, vnb[rows, cols]) + gbs[:, cols]
            yg_ref[rows, cols] = u[rows, cols] * s
    y_gm = yg_ref[...]

    y = _merge_out(hb, win_ref, y_pool, y_lru, y_gm, wbop_ref, wbol_ref, wbog_ref, wout_ref)
    o_ref[...] = x + gt_ref[...] * y.reshape(1, tm, D)


def _mix_weight_specs(l):
    return [
        _row_spec(l, D),
        _full_spec(l, (D, IN_W)),
        _full_spec(l, (len(POOL_WINDOWS), POOL_GW, POOL_GW)),
        _row_spec(l, D),
        pl.BlockSpec((None, CONV_W, D), lambda i, j: (l, 0, 0)),
        _row_spec(l, D),
        _full_spec(l, (LRU_HEADS, LRU_HD, 2 * LRU_HD)),
        _row_spec(l, D), _row_spec(l, D), _row_spec(l, D),
        _row_spec(l, GMLP_W),
    ]


def _mix_proj_specs(l):
    return [_full_spec(l, (D, D)), _full_spec(l, (D, D)), _full_spec(l, (GMLP_W, D)),
            _full_spec(l, (D, D))]


def _mix_prompt(x, mods, l, wts, gws, gbs_rows, *, tm):
    nbt, tt, _ = x.shape
    grid = (nbt, tt // tm)
    xspec = pl.BlockSpec((1, tm, D), lambda i, j: (i, j, 0))
    in_specs = ([xspec, _mod_spec(l, 3, 1), _mod_spec(l, 4, 1), _mod_spec(l, 5, 1)]
                + _mix_weight_specs(l)
                + [pl.BlockSpec((None, GMLP_W // GMLP_GW, CHUNK, CHUNK), lambda i, j: (l, 0, 0, 0)),
                   pl.BlockSpec((None, CHUNK, GMLP_W), lambda i, j: (l, 0, 0))]
                + _mix_proj_specs(l))
    out_specs = [
        xspec,
        pl.BlockSpec((None, POOL_PREFIX, D), lambda i, j: (i, 0, 0)),
        pl.BlockSpec((None, CONV_W - 1, D), lambda i, j: (i, 0, 0)),
        pl.BlockSpec((None, 1, D), lambda i, j: (i, 0, 0)),
    ]
    out_shape = [
        jax.ShapeDtypeStruct(x.shape, F32),
        jax.ShapeDtypeStruct((nbt, POOL_PREFIX, D), F32),
        jax.ShapeDtypeStruct((nbt, CONV_W - 1, D), F32),
        jax.ShapeDtypeStruct((nbt, 1, D), F32),
    ]
    scratch = [
        pltpu.VMEM((PREF + tm, D), F32),
        pltpu.VMEM((PREF + tm, POOL_GW), F32),
        pltpu.VMEM((PREF + tm, POOL_GW), F32),
        pltpu.VMEM((CPREF + tm, D), F32),
        pltpu.VMEM((tm, D), F32),
        pltpu.VMEM((tm, D), F32),
        pltpu.VMEM((tm, D), F32),
        pltpu.VMEM((1, D), F32),
        pltpu.VMEM((tm, GMLP_W), F32),
    ]
    (norm, w_in, pool_w, pool_scale, conv_w, conv_b, wri, br, bi, lam, gnorm,
     wbop, wbol, wbog, wout) = wts
    return pl.pallas_call(
        functools.partial(_mix_prompt_body, tm=tm),
        grid=grid,
        in_specs=in_specs,
        out_specs=out_specs,
        out_shape=out_shape,
        scratch_shapes=scratch,
        compiler_params=pltpu.CompilerParams(
            dimension_semantics=("arbitrary", "arbitrary"), vmem_limit_bytes=VMEM_LIMIT),
        name="mix_prompt",
    )(x, mods, mods, mods, norm, w_in, pool_w, pool_scale, conv_w, conv_b, wri, br, bi, lam, gnorm,
      gws, gbs_rows, wbop, wbol, wbog, wout)


def _mix_sample_body(x_ref, sh_ref, sc_ref, gt_ref, nm_ref, win_ref, poolw_ref, pscale_ref,
                     convw_ref, convb_ref, wri_ref, br_ref, bi_ref, lam_ref, gnorm_ref, gws_ref,
                     gbs_ref, wbop_ref, wbol_ref, wbog_ref, wout_ref,
                     spool_ref, sconv_ref, sh0_ref,
                     o_ref, npool_ref, nconv_ref, nh_ref, vn_ref,
                     p_ref, c_ref, *, nb, t):
    rows = nb * t
    x = x_ref[...]
    h = _norm_mod(x, nm_ref[...], sc_ref[...], sh_ref[...])
    hb = h.reshape(rows, D).astype(BF16)

    xp = _dot(hb, win_ref[:, 0:D]).reshape(nb, t, D)
    p_ref[:, pl.ds(1, POOL_PREFIX), :] = spool_ref[...]
    p_ref[:, pl.ds(POOL_PREFIX + 1, t), :] = xp
    d_parts = []
    for g, w in enumerate(POOL_WINDOWS):
        cols = slice(g * POOL_GW, (g + 1) * POOL_GW)
        s = xp[:, :, cols]
        for k in range(1, w):
            s = s + p_ref[:, pl.ds(POOL_PREFIX + 1 - k, t), cols]
        d_parts.append((s / float(w) - xp[:, :, cols]).reshape(rows, POOL_GW))
    y_pool = _pool_project(d_parts, poolw_ref, pscale_ref[...])
    npool_ref[...] = p_ref[:, pl.ds(t + 1, POOL_PREFIX), :]

    xl = _dot(hb, win_ref[:, D:2 * D]).reshape(nb, t, D)
    c_ref[:, pl.ds(CPREF - (CONV_W - 1), CONV_W - 1), :] = sconv_ref[...]
    c_ref[:, pl.ds(CPREF, t), :] = xl
    cw = convw_ref[...]
    xc = convb_ref[...] + cw[3:4, :] * xl
    for k in range(CONV_W - 1):
        xc = xc + cw[k:k + 1, :] * c_ref[:, pl.ds(CPREF - (CONV_W - 1) + k, t), :]
    nconv_ref[...] = c_ref[:, pl.ds(CPREF + t - (CONV_W - 1), CONV_W - 1), :]

    a, bx = _lru_coeffs(xc.reshape(rows, D), wri_ref, br_ref[...], bi_ref[...], lam_ref[...])
    a, bx = _scan8(a, bx)
    hs = a.reshape(nb, t, D) * sh0_ref[...] + bx.reshape(nb, t, D)
    nh_ref[...] = hs[:, t - 1:t, :]
    gl = _dot(hb, win_ref[:, 2 * D:3 * D])
    y_lru = hs.reshape(rows, D) * jax.nn.gelu(gl)

    u = _dot(hb, win_ref[:, 3 * D:3 * D + GMLP_W])
    v = _dot(hb, win_ref[:, 3 * D + GMLP_W:4 * D])
    vn = _rms(v, gnorm_ref[...]).reshape(nb, t, GMLP_W)
    vn_ref[...] = vn
    tri = (lax.broadcasted_iota(jnp.int32, (t, t), 0) >= lax.broadcasted_iota(jnp.int32, (t, t), 1))
    gbs = gbs_ref[...]
    s_parts = []
    for g in range(GMLP_W // GMLP_GW):
        cols = slice(g * GMLP_GW, (g + 1) * GMLP_GW)
        wm = jnp.where(tri, gws_ref[g][0:t, 0:t], 0.0)
        s = jnp.broadcast_to(gbs[0:t, cols], (nb, t, GMLP_GW))
        for k in range(t):
            s = s + wm[:, k:k + 1] * vn[:, k:k + 1, cols]
        s_parts.append(s)
    y_gm = u * jnp.concatenate(s_parts, axis=2).reshape(rows, GMLP_W)

    y = _merge_out(hb, win_ref, y_pool, y_lru, y_gm, wbop_ref, wbol_ref, wbog_ref, wout_ref)
    o_ref[...] = x + gt_ref[...] * y.reshape(nb, t, D)


def _mix_sample(x, mods, l, wts, gws, gbs_rows, spool, sconv, sh0, *, nb):
    nbt, t, _ = x.shape
    grid = (nbt // nb, 1)
    xspec = pl.BlockSpec((nb, t, D), lambda i, j: (i, 0, 0))

    def state_spec(r):
        return pl.BlockSpec((None, nb, r, D), lambda i, j: (l, i, 0, 0))

    def out_state_spec(r):
        return pl.BlockSpec((nb, r, D), lambda i, j: (i, 0, 0))

    in_specs = ([xspec, _mod_spec(l, 3, nb), _mod_spec(l, 4, nb), _mod_spec(l, 5, nb)]
                + _mix_weight_specs(l)
                + [pl.BlockSpec((None, GMLP_W // GMLP_GW, CHUNK, CHUNK), lambda i, j: (l, 0, 0, 0)),
                   pl.BlockSpec((None, CHUNK, GMLP_W), lambda i, j: (l, 0, 0))]
                + _mix_proj_specs(l)
                + [state_spec(POOL_PREFIX), state_spec(CONV_W - 1), state_spec(1)])
    out_specs = [xspec, out_state_spec(POOL_PREFIX), out_state_spec(CONV_W - 1), out_state_spec(1),
                 pl.BlockSpec((nb, t, GMLP_W), lambda i, j: (i, 0, 0))]
    out_shape = [
        jax.ShapeDtypeStruct(x.shape, F32),
        jax.ShapeDtypeStruct((nbt, POOL_PREFIX, D), F32),
        jax.ShapeDtypeStruct((nbt, CONV_W - 1, D), F32),
        jax.ShapeDtypeStruct((nbt, 1, D), F32),
        jax.ShapeDtypeStruct((nbt, t, GMLP_W), F32),
    ]
    scratch = [
        pltpu.VMEM((nb, POOL_PREFIX + 1 + t, D), F32),
        pltpu.VMEM((nb, CPREF + t, D), F32),
    ]
    (norm, w_in, pool_w, pool_scale, conv_w, conv_b, wri, br, bi, lam, gnorm,
     wbop, wbol, wbog, wout) = wts
    return pl.pallas_call(
        functools.partial(_mix_sample_body, nb=nb, t=t),
        grid=grid,
        in_specs=in_specs,
        out_specs=out_specs,
        out_shape=out_shape,
        scratch_shapes=scratch,
        compiler_params=pltpu.CompilerParams(
            dimension_semantics=("arbitrary", "arbitrary"), vmem_limit_bytes=VMEM_LIMIT),
        name="mix_sample",
    )(x, mods, mods, mods, norm, w_in, pool_w, pool_scale, conv_w, conv_b, wri, br, bi, lam, gnorm,
      gws, gbs_rows, wbop, wbol, wbog, wout, spool, sconv, sh0)


def kernel(x_prompt, x_sample, c_prompt, c_sample, state_pool, state_conv, state_lru, w_ada, b_ada,
           norm_ffn1, ffn1_w_gate, ffn1_w_up, ffn1_w_down, norm_mix, w_in, pool_w, pool_scale,
           conv_w, conv_b, lru_wr, lru_br, lru_wi, lru_bi, lru_lambda, gmlp_norm, gmlp_ws, gmlp_bs,
           wbo_pool, wbo_lru, wbo_gmlp, w_out, norm_ffn2, ffn2_w_gate, ffn2_w_up, ffn2_w_down,
           norm_final):
    nbp, tp, _ = x_prompt.shape
    nbs, ts, _ = x_sample.shape
    tm_ffn, tm_mix, nb_s = 512, 256, 32

    mods_p, mods_s = _ada(c_prompt, c_sample, w_ada, b_ada)
    mods_p = mods_p.reshape(DEPTH, N_MOD, nbp, 1, D)
    mods_s = mods_s.reshape(DEPTH, N_MOD, nbs, 1, D)

    def row(v):
        return v.reshape(DEPTH, 1, v.shape[-1])

    bf = lambda w: w.astype(BF16)
    f1 = (row(norm_ffn1), bf(ffn1_w_gate), bf(ffn1_w_up), bf(ffn1_w_down))
    f2 = (row(norm_ffn2), bf(ffn2_w_gate), bf(ffn2_w_up), bf(ffn2_w_down))
    wri = bf(jnp.concatenate([lru_wr, lru_wi], axis=-1))
    mixw = (row(norm_mix), bf(w_in), bf(pool_w), row(pool_scale), conv_w, row(conv_b), wri,
            row(lru_br), row(lru_bi), row(lru_lambda), row(gmlp_norm),
            bf(wbo_pool), bf(wbo_lru), bf(wbo_gmlp), bf(w_out))
    gbs_rows = jnp.repeat(jnp.swapaxes(gmlp_bs, 1, 2), GMLP_GW, axis=2)
    nf = norm_final.reshape(1, D)
    sh0 = state_lru.reshape(DEPTH, nbs, 1, D)

    xp, xs = x_prompt, x_sample
    pp, pc, ph, sp, sc, sh, sv = [], [], [], [], [], [], []
    for l in range(DEPTH):
        final = l == DEPTH - 1
        xp = _ffn(xp, mods_p, l, 0, *f1, nf, nb=1, t=tm_ffn, final=False)
        xp, npool, nconv, nh = _mix_prompt(xp, mods_p, l, mixw, gmlp_ws, gbs_rows, tm=tm_mix)
        xp = _ffn(xp, mods_p, l, 6, *f2, nf, nb=1, t=tm_ffn, final=final)
        pp.append(npool); pc.append(nconv); ph.append(nh.reshape(nbp, D))

        xs = _ffn(xs, mods_s, l, 0, *f1, nf, nb=nb_s, t=ts, final=False)
        xs, npool, nconv, nh, nv = _mix_sample(xs, mods_s, l, mixw, gmlp_ws, gbs_rows,
                                               state_pool, state_conv, sh0, nb=nb_s)
        xs = _ffn(xs, mods_s, l, 6, *f2, nf, nb=nb_s, t=ts, final=final)
        sp.append(npool); sc.append(nconv); sh.append(nh.reshape(nbs, D)); sv.append(nv)
    return (xp, xs, jnp.stack(pp), jnp.stack(pc), jnp.stack(ph),
            jnp.stack(sp), jnp.stack(sc), jnp.stack(sh), jnp.stack(sv))
```

```python
import functools

import numpy as np
import jax
import jax.numpy as jnp
from jax import lax
from jax.experimental import pallas as pl
from jax.experimental.pallas import tpu as pltpu

D = 1024
DEPTH = 4
PAST_LEN = 16384
POOL_WINDOWS = (2, 4, 8, 16)
POOL_GW = 256
POOL_PREFIX = 15
LRU_HEADS = 8
LRU_HD = 128
CONV_W = 4
LRU_C = 8.0
GMLP_W = 512
GMLP_GW = 128
CHUNK = 128
IN_W = 7168
D_FF = 2816
N_MOD = 9
EPS = 1e-6

FF_CHUNK = 256
SUBLANES = 8
SEG = CHUNK // SUBLANES
CPREF = 8
VMEM_LIMIT = 56 * 1024 * 1024

F32 = jnp.float32
BF16 = jnp.bfloat16


def _dot(a, b):
    return jnp.dot(a, b, preferred_element_type=F32)


def _norm_mod(x, n, sc, sh):
    ms = jnp.mean(x * x, axis=-1, keepdims=True)
    y = x * lax.rsqrt(ms + EPS) * n
    return y * (1.0 + sc) + sh


def _ada_body(cp_ref, cs_ref, w_ref, b_ref, op_ref, os_ref):
    w = w_ref[...].astype(BF16)
    b = b_ref[...]

    def f(c):
        s = c * jax.nn.sigmoid(c)
        return _dot(s.astype(BF16), w) + b

    op_ref[...] = f(cp_ref[...])
    os_ref[...] = f(cs_ref[...])


def _ada(c_prompt, c_sample, w_ada, b_ada):
    nbp, nbs = c_prompt.shape[0], c_sample.shape[0]
    return pl.pallas_call(
        _ada_body,
        grid=(DEPTH, N_MOD),
        in_specs=[
            pl.BlockSpec((nbp, D), lambda l, j: (0, 0)),
            pl.BlockSpec((nbs, D), lambda l, j: (0, 0)),
            pl.BlockSpec((None, D, D), lambda l, j: (l, 0, j)),
            pl.BlockSpec((None, 1, D), lambda l, j: (l, 0, j)),
        ],
        out_specs=[
            pl.BlockSpec((None, None, nbp, D), lambda l, j: (l, j, 0, 0)),
            pl.BlockSpec((None, None, nbs, D), lambda l, j: (l, j, 0, 0)),
        ],
        out_shape=[
            jax.ShapeDtypeStruct((DEPTH, N_MOD, nbp, D), F32),
            jax.ShapeDtypeStruct((DEPTH, N_MOD, nbs, D), F32),
        ],
        name="ada",
    )(c_prompt, c_sample, w_ada, b_ada.reshape(DEPTH, 1, N_MOD * D))


def _ffn_body(x_ref, sh_ref, sc_ref, gt_ref, n_ref, wg_ref, wu_ref, wd_ref, nf_ref, o_ref, *, final):
    x = x_ref[...]
    nb, t, _ = x.shape
    rows = nb * t
    h = _norm_mod(x, n_ref[...], sc_ref[...], sh_ref[...])
    hb = h.reshape(rows, D).astype(BF16)
    acc = jnp.zeros((rows, D), F32)
    for c in range(D_FF // FF_CHUNK):
        sl = slice(c * FF_CHUNK, (c + 1) * FF_CHUNK)
        g = _dot(hb, wg_ref[:, sl])
        u = _dot(hb, wu_ref[:, sl])
        a = (g * jax.nn.sigmoid(g)) * u
        acc = acc + _dot(a.astype(BF16), wd_ref[sl, :])
    out = x + (0.5 * gt_ref[...]) * acc.reshape(nb, t, D)
    if final:
        ms = jnp.mean(out * out, axis=-1, keepdims=True)
        out = out * lax.rsqrt(ms + EPS) * nf_ref[...]
    o_ref[...] = out


def _mod_spec(l, k, nb):
    return pl.BlockSpec((None, None, nb, 1, D), lambda i, j: (l, k, i, 0, 0))


def _row_spec(l, width):
    return pl.BlockSpec((None, 1, width), lambda i, j: (l, 0, 0))


def _full_spec(l, shape):
    nd = len(shape)
    return pl.BlockSpec((None,) + tuple(shape), lambda i, j: (l,) + (0,) * nd,
                        pipeline_mode=pl.Buffered(1))


def _ffn(x, mods, l, k0, norm, wg, wu, wd, norm_final, *, nb, t, final):
    nbt, tt, _ = x.shape
    grid = (nbt // nb, tt // t)
    xspec = pl.BlockSpec((nb, t, D), lambda i, j: (i, j, 0))
    return pl.pallas_call(
        functools.partial(_ffn_body, final=final),
        grid=grid,
        in_specs=[
            xspec,
            _mod_spec(l, k0, nb), _mod_spec(l, k0 + 1, nb), _mod_spec(l, k0 + 2, nb),
            _row_spec(l, D),
            _full_spec(l, (D, D_FF)), _full_spec(l, (D, D_FF)), _full_spec(l, (D_FF, D)),
            pl.BlockSpec((1, D), lambda i, j: (0, 0)),
        ],
        out_specs=xspec,
        out_shape=jax.ShapeDtypeStruct(x.shape, F32),
        compiler_params=pltpu.CompilerParams(
            dimension_semantics=("arbitrary", "arbitrary"), vmem_limit_bytes=VMEM_LIMIT),
        name="ffn_final" if final else "ffn",
    )(x, mods, mods, mods, norm, wg, wu, wd, norm_final)


def _scan8(a, b):
    row = lax.broadcasted_iota(jnp.int32, a.shape, 0) & (SUBLANES - 1)
    for d in (1, 2, 4):
        m = row >= d
        a_s = jnp.where(m, pltpu.roll(a, d, 0), 1.0)
        b_s = jnp.where(m, pltpu.roll(b, d, 0), 0.0)
        b = a * b_s + b
        a = a * a_s
    return a, b


def _lru_coeffs(xc, wri_ref, br, bi, lam):
    xcb = xc.astype(BF16)
    rs, is_ = [], []
    for hd in range(LRU_HEADS):
        res = _dot(xcb[:, hd * LRU_HD:(hd + 1) * LRU_HD], wri_ref[hd])
        rs.append(res[:, :LRU_HD])
        is_.append(res[:, LRU_HD:])
    r = jax.nn.sigmoid(jnp.concatenate(rs, axis=1) + br)
    i = jax.nn.sigmoid(jnp.concatenate(is_, axis=1) + bi)
    nl = -lam
    softplus = jnp.maximum(nl, 0.0) + jnp.log1p(jnp.exp(-jnp.abs(nl)))
    log_a = (-LRU_C * r) * softplus
    a = jnp.exp(log_a)
    bx = jnp.sqrt(1.0 - a * a) * (i * xc)
    return a, bx


def _pool_project(d_parts, poolw_ref, pscale):
    ys = [_dot(d.astype(BF16), poolw_ref[g]) for g, d in enumerate(d_parts)]
    return jnp.concatenate(ys, axis=1) * pscale


def _rms(v, g):
    ms = jnp.mean(v * v, axis=-1, keepdims=True)
    return v * lax.rsqrt(ms + EPS) * g


def _merge_out(hb, win_ref, y_pool, y_lru, y_gm, wbop_ref, wbol_ref, wbog_ref, wout_ref):
    def gate(k):
        c0 = D + 2 * D + 2 * GMLP_W + k * D
        return jax.nn.sigmoid(_dot(hb, win_ref[:, c0:c0 + D]))

    merged = gate(0) * _dot(y_pool.astype(BF16), wbop_ref[...])
    merged = merged + gate(1) * _dot(y_lru.astype(BF16), wbol_ref[...])
    merged = merged + gate(2) * _dot(y_gm.astype(BF16), wbog_ref[...])
    return _dot(merged.astype(BF16), wout_ref[...])


def _extend(x, carry_ref, e_ref):
    tm, cdim = x.shape
    nc = tm // CHUNK
    xr = pltpu.roll(x.reshape(tm // SUBLANES, SUBLANES, cdim), 1, 1).reshape(tm, cdim)
    prev = carry_ref[...]
    if nc > 1:
        prev = jnp.concatenate([prev, xr[:tm - CHUNK]], axis=0)
    sub0 = (lax.broadcasted_iota(jnp.int32, (tm, cdim), 0) & (SUBLANES - 1)) == 0
    e_ref[:, 0:CHUNK, :] = jnp.where(sub0, prev, xr).reshape(nc, CHUNK, cdim)
    e_ref[:, CHUNK:2 * CHUNK, :] = x.reshape(nc, CHUNK, cdim)
    carry_ref[...] = xr[tm - CHUNK:]


def _shifted(e_ref, k, cols=slice(None)):
    v = e_ref[:, pl.ds((SEG - k) * SUBLANES, CHUNK), cols]
    return v.reshape(v.shape[0] * CHUNK, v.shape[2])


def _window_sum(e_ref, ta_ref, tb_ref, c0, w):
    src, cols = e_ref, slice(c0, c0 + POOL_GW)
    bufs = (ta_ref, tb_ref)
    k, level, lo = 1, 0, 0
    while True:
        last = 2 * k == w
        lo = lo + k
        start = SEG if last else lo
        n = (2 * SEG - start) * SUBLANES
        val = (src[:, pl.ds(start * SUBLANES, n), cols]
               + src[:, pl.ds((start - k) * SUBLANES, n), cols])
        if last:
            return val.reshape(val.shape[0] * CHUNK, POOL_GW)
        dst = bufs[level % 2]
        dst[:, pl.ds(start * SUBLANES, n), :] = val
        src, cols = dst, slice(0, POOL_GW)
        k, level = 2 * k, level + 1


def _stored_time(shape, axis):
    r = lax.broadcasted_iota(jnp.int32, shape, axis)
    return (r & (SUBLANES - 1)) * SEG + ((r >> 3) & (SEG - 1))


def _mix_prompt_body(x_ref, sh_ref, sc_ref, gt_ref, nm_ref, win_ref, poolw_ref, pscale_ref,
                     convw_ref, convb_ref, wri_ref, br_ref, bi_ref, lam_ref, gnorm_ref, gws_ref,
                     gbs_ref, wbop_ref, wbol_ref, wbog_ref, wout_ref,
                     o_ref, tailp_ref, tailc_ref, nh_ref,
                     ep_ref, cp_ref, ta_ref, tb_ref, ec_ref, cc_ref, pa_ref, h_ref, hs_ref, hc_ref,
                     yg_ref, *, tm):
    j = pl.program_id(1)
    nt = pl.num_programs(1)
    nc = tm // CHUNK

    @pl.when(j == 0)
    def _():
        cp_ref[...] = jnp.zeros((CHUNK, D), F32)
        cc_ref[...] = jnp.zeros((CHUNK, D), F32)
        hc_ref[...] = jnp.zeros((1, D), F32)

    x = x_ref[...]
    h = _norm_mod(x, nm_ref[...], sc_ref[...], sh_ref[...])
    hb = h.reshape(tm, D).astype(BF16)

    xp = _dot(hb, win_ref[:, 0:D])
    _extend(xp, cp_ref, ep_ref)
    row = lax.broadcasted_iota(jnp.int32, (tm, POOL_GW), 0)
    pos = j * tm + (row & ~(CHUNK - 1)) + _stored_time((tm, POOL_GW), 0)
    d_parts = []
    for g, w in enumerate(POOL_WINDOWS):
        s = _window_sum(ep_ref, ta_ref, tb_ref, g * POOL_GW, w)
        cnt = jnp.minimum(w, pos + 1).astype(F32)
        d_parts.append(s / cnt - xp[:, g * POOL_GW:(g + 1) * POOL_GW])
    y_pool = _pool_project(d_parts, poolw_ref, pscale_ref[...])

    xl = _dot(hb, win_ref[:, D:2 * D])
    _extend(xl, cc_ref, ec_ref)
    cw = convw_ref[...]
    xc = convb_ref[...] + cw[CONV_W - 1:CONV_W, :] * xl
    for k in range(CONV_W - 1):
        xc = xc + cw[k:k + 1, :] * _shifted(ec_ref, CONV_W - 1 - k)

    @pl.when(j == nt - 1)
    def _():
        tailp_ref[...] = xp[tm - CHUNK:, :]
        tailc_ref[...] = xl[tm - CHUNK:, :]

    a, bx = _lru_coeffs(xc, wri_ref, br_ref[...], bi_ref[...], lam_ref[...])
    a3 = a.reshape(nc, CHUNK, D)
    b3 = bx.reshape(nc, CHUNK, D)
    hq = b3[:, 0:SUBLANES, :]
    paq = a3[:, 0:SUBLANES, :]
    pa_ref[:, 0:SUBLANES, :] = paq
    h_ref[:, 0:SUBLANES, :] = hq
    for q in range(1, SEG):
        rows = slice(q * SUBLANES, (q + 1) * SUBLANES)
        hq = a3[:, rows, :] * hq + b3[:, rows, :]
        paq = a3[:, rows, :] * paq
        pa_ref[:, rows, :] = paq
        h_ref[:, rows, :] = hq
    g_state = hc_ref[...]
    entry = []
    for c in range(nc):
        rows_c = []
        for s in range(SUBLANES):
            rows_c.append(g_state)
            g_state = paq[c, s:s + 1, :] * g_state + hq[c, s:s + 1, :]
        entry.append(jnp.concatenate(rows_c, axis=0))
    entry = jnp.stack(entry, axis=0)
    hc_ref[...] = g_state

    @pl.when(j == nt - 1)
    def _():
        nh_ref[...] = g_state

    for q in range(SEG):
        rows = slice(q * SUBLANES, (q + 1) * SUBLANES)
        hs_ref[:, rows, :] = pa_ref[:, rows, :] * entry + h_ref[:, rows, :]

    gl = _dot(hb, win_ref[:, 2 * D:3 * D])
    y_lru = hs_ref[...].reshape(tm, D) * jax.nn.gelu(gl)

    u = _dot(hb, win_ref[:, 3 * D:3 * D + GMLP_W])
    v = _dot(hb, win_ref[:, 3 * D + GMLP_W:4 * D])
    vnb = _rms(v, gnorm_ref[...]).astype(BF16)
    tri = _stored_time((CHUNK, CHUNK), 0) >= _stored_time((CHUNK, CHUNK), 1)
    gbs = gbs_ref[...]
    for g in range(GMLP_W // GMLP_GW):
        wm = jnp.where(tri, gws_ref[g], 0.0).astype(BF16)
        cols = slice(g * GMLP_GW, (g + 1) * GMLP_GW)
        for c in range(nc):
            rows = slice(c * CHUNK, (c + 1) * CHUNK)
            s = _dot(wm, vnb[rows, cols]) + gbs[:, cols]
            yg_ref[rows, cols] = u[rows, cols] * s
    y_gm = yg_ref[...]

    y = _merge_out(hb, win_ref, y_pool, y_lru, y_gm, wbop_ref, wbol_ref, wbog_ref, wout_ref)
    o_ref[...] = x + gt_ref[...] * y.reshape(1, tm, D)


def _mix_weight_specs(l):
    return [
        _row_spec(l, D),
        _full_spec(l, (D, IN_W)),
        _full_spec(l, (len(POOL_WINDOWS), POOL_GW, POOL_GW)),
        _row_spec(l, D),
        pl.BlockSpec((None, CONV_W, D), lambda i, j: (l, 0, 0)),
        _row_spec(l, D),
        _full_spec(l, (LRU_HEADS, LRU_HD, 2 * LRU_HD)),
        _row_spec(l, D), _row_spec(l, D), _row_spec(l, D),
        _row_spec(l, GMLP_W),
    ]


def _mix_proj_specs(l):
    return [_full_spec(l, (D, D)), _full_spec(l, (D, D)), _full_spec(l, (GMLP_W, D)),
            _full_spec(l, (D, D))]


def _mix_prompt(x, mods, l, wts, gws, gbs_rows, *, tm):
    nbt, tt, _ = x.shape
    grid = (nbt, tt // tm)
    xspec = pl.BlockSpec((1, tm, D), lambda i, j: (i, j, 0))
    in_specs = ([xspec, _mod_spec(l, 3, 1), _mod_spec(l, 4, 1), _mod_spec(l, 5, 1)]
                + _mix_weight_specs(l)
                + [pl.BlockSpec((None, GMLP_W // GMLP_GW, CHUNK, CHUNK), lambda i, j: (l, 0, 0, 0)),
                   pl.BlockSpec((None, CHUNK, GMLP_W), lambda i, j: (l, 0, 0))]
                + _mix_proj_specs(l))
    nc = tm // CHUNK
    out_specs = [
        xspec,
        pl.BlockSpec((None, CHUNK, D), lambda i, j: (i, 0, 0)),
        pl.BlockSpec((None, CHUNK, D), lambda i, j: (i, 0, 0)),
        pl.BlockSpec((None, 1, D), lambda i, j: (i, 0, 0)),
    ]
    out_shape = [
        jax.ShapeDtypeStruct(x.shape, F32),
        jax.ShapeDtypeStruct((nbt, CHUNK, D), F32),
        jax.ShapeDtypeStruct((nbt, CHUNK, D), F32),
        jax.ShapeDtypeStruct((nbt, 1, D), F32),
    ]
    scratch = [
        pltpu.VMEM((nc, 2 * CHUNK, D), F32),
        pltpu.VMEM((CHUNK, D), F32),
        pltpu.VMEM((nc, 2 * CHUNK, POOL_GW), F32),
        pltpu.VMEM((nc, 2 * CHUNK, POOL_GW), F32),
        pltpu.VMEM((nc, 2 * CHUNK, D), F32),
        pltpu.VMEM((CHUNK, D), F32),
        pltpu.VMEM((nc, CHUNK, D), F32),
        pltpu.VMEM((nc, CHUNK, D), F32),
        pltpu.VMEM((nc, CHUNK, D), F32),
        pltpu.VMEM((1, D), F32),
        pltpu.VMEM((tm, GMLP_W), F32),
    ]
    (norm, w_in, pool_w, pool_scale, conv_w, conv_b, wri, br, bi, lam, gnorm,
     wbop, wbol, wbog, wout) = wts
    return pl.pallas_call(
        functools.partial(_mix_prompt_body, tm=tm),
        grid=grid,
        in_specs=in_specs,
        out_specs=out_specs,
        out_shape=out_shape,
        scratch_shapes=scratch,
        compiler_params=pltpu.CompilerParams(
            dimension_semantics=("arbitrary", "arbitrary"), vmem_limit_bytes=VMEM_LIMIT),
        name="mix_prompt",
    )(x, mods, mods, mods, norm, w_in, pool_w, pool_scale, conv_w, conv_b, wri, br, bi, lam, gnorm,
      gws, gbs_rows, wbop, wbol, wbog, wout)


def _mix_sample_body(x_ref, sh_ref, sc_ref, gt_ref, nm_ref, win_ref, poolw_ref, pscale_ref,
                     convw_ref, convb_ref, wri_ref, br_ref, bi_ref, lam_ref, gnorm_ref, gws_ref,
                     gbs_ref, wbop_ref, wbol_ref, wbog_ref, wout_ref,
                     spool_ref, sconv_ref, sh0_ref,
                     o_ref, npool_ref, nconv_ref, nh_ref, vn_ref,
                     p_ref, c_ref, *, nb, t):
    rows = nb * t
    x = x_ref[...]
    h = _norm_mod(x, nm_ref[...], sc_ref[...], sh_ref[...])
    hb = h.reshape(rows, D).astype(BF16)

    xp = _dot(hb, win_ref[:, 0:D]).reshape(nb, t, D)
    p_ref[:, pl.ds(1, POOL_PREFIX), :] = spool_ref[...]
    p_ref[:, pl.ds(POOL_PREFIX + 1, t), :] = xp
    d_parts = []
    for g, w in enumerate(POOL_WINDOWS):
        cols = slice(g * POOL_GW, (g + 1) * POOL_GW)
        s = xp[:, :, cols]
        for k in range(1, w):
            s = s + p_ref[:, pl.ds(POOL_PREFIX + 1 - k, t), cols]
        d_parts.append((s / float(w) - xp[:, :, cols]).reshape(rows, POOL_GW))
    y_pool = _pool_project(d_parts, poolw_ref, pscale_ref[...])
    npool_ref[...] = p_ref[:, pl.ds(t + 1, POOL_PREFIX), :]

    xl = _dot(hb, win_ref[:, D:2 * D]).reshape(nb, t, D)
    c_ref[:, pl.ds(CPREF - (CONV_W - 1), CONV_W - 1), :] = sconv_ref[...]
    c_ref[:, pl.ds(CPREF, t), :] = xl
    cw = convw_ref[...]
    xc = convb_ref[...] + cw[3:4, :] * xl
    for k in range(CONV_W - 1):
        xc = xc + cw[k:k + 1, :] * c_ref[:, pl.ds(CPREF - (CONV_W - 1) + k, t), :]
    nconv_ref[...] = c_ref[:, pl.ds(CPREF + t - (CONV_W - 1), CONV_W - 1), :]

    a, bx = _lru_coeffs(xc.reshape(rows, D), wri_ref, br_ref[...], bi_ref[...], lam_ref[...])
    a, bx = _scan8(a, bx)
    hs = a.reshape(nb, t, D) * sh0_ref[...] + bx.reshape(nb, t, D)
    nh_ref[...] = hs[:, t - 1:t, :]
    gl = _dot(hb, win_ref[:, 2 * D:3 * D])
    y_lru = hs.reshape(rows, D) * jax.nn.gelu(gl)

    u = _dot(hb, win_ref[:, 3 * D:3 * D + GMLP_W])
    v = _dot(hb, win_ref[:, 3 * D + GMLP_W:4 * D])
    vn = _rms(v, gnorm_ref[...]).reshape(nb, t, GMLP_W)
    vn_ref[...] = vn
    tri = (lax.broadcasted_iota(jnp.int32, (t, t), 0) >= lax.broadcasted_iota(jnp.int32, (t, t), 1))
    gbs = gbs_ref[...]
    s_parts = []
    for g in range(GMLP_W // GMLP_GW):
        cols = slice(g * GMLP_GW, (g + 1) * GMLP_GW)
        wm = jnp.where(tri, gws_ref[g][0:t, 0:t], 0.0)
        s = jnp.broadcast_to(gbs[0:t, cols], (nb, t, GMLP_GW))
        for k in range(t):
            s = s + wm[:, k:k + 1] * vn[:, k:k + 1, cols]
        s_parts.append(s)
    y_gm = u * jnp.concatenate(s_parts, axis=2).reshape(rows, GMLP_W)

    y = _merge_out(hb, win_ref, y_pool, y_lru, y_gm, wbop_ref, wbol_ref, wbog_ref, wout_ref)
    o_ref[...] = x + gt_ref[...] * y.reshape(nb, t, D)


def _mix_sample(x, mods, l, wts, gws, gbs_rows, spool, sconv, sh0, *, nb):
    nbt, t, _ = x.shape
    grid = (nbt // nb, 1)
    xspec = pl.BlockSpec((nb, t, D), lambda i, j: (i, 0, 0))

    def state_spec(r):
        return pl.BlockSpec((None, nb, r, D), lambda i, j: (l, i, 0, 0))

    def out_state_spec(r):
        return pl.BlockSpec((nb, r, D), lambda i, j: (i, 0, 0))

    in_specs = ([xspec, _mod_spec(l, 3, nb), _mod_spec(l, 4, nb), _mod_spec(l, 5, nb)]
                + _mix_weight_specs(l)
                + [pl.BlockSpec((None, GMLP_W // GMLP_GW, CHUNK, CHUNK), lambda i, j: (l, 0, 0, 0)),
                   pl.BlockSpec((None, CHUNK, GMLP_W), lambda i, j: (l, 0, 0))]
                + _mix_proj_specs(l)
                + [state_spec(POOL_PREFIX), state_spec(CONV_W - 1), state_spec(1)])
    out_specs = [xspec, out_state_spec(POOL_PREFIX), out_state_spec(CONV_W - 1), out_state_spec(1),
                 pl.BlockSpec((nb, t, GMLP_W), lambda i, j: (i, 0, 0))]
    out_shape = [
        jax.ShapeDtypeStruct(x.shape, F32),
        jax.ShapeDtypeStruct((nbt, POOL_PREFIX, D), F32),
        jax.ShapeDtypeStruct((nbt, CONV_W - 1, D), F32),
        jax.ShapeDtypeStruct((nbt, 1, D), F32),
        jax.ShapeDtypeStruct((nbt, t, GMLP_W), F32),
    ]
    scratch = [
        pltpu.VMEM((nb, POOL_PREFIX + 1 + t, D), F32),
        pltpu.VMEM((nb, CPREF + t, D), F32),
    ]
    (norm, w_in, pool_w, pool_scale, conv_w, conv_b, wri, br, bi, lam, gnorm,
     wbop, wbol, wbog, wout) = wts
    return pl.pallas_call(
        functools.partial(_mix_sample_body, nb=nb, t=t),
        grid=grid,
        in_specs=in_specs,
        out_specs=out_specs,
        out_shape=out_shape,
        scratch_shapes=scratch,
        compiler_params=pltpu.CompilerParams(
            dimension_semantics=("arbitrary", "arbitrary"), vmem_limit_bytes=VMEM_LIMIT),
        name="mix_sample",
    )(x, mods, mods, mods, norm, w_in, pool_w, pool_scale, conv_w, conv_b, wri, br, bi, lam, gnorm,
      gws, gbs_rows, wbop, wbol, wbog, wout, spool, sconv, sh0)


def kernel(x_prompt, x_sample, c_prompt, c_sample, state_pool, state_conv, state_lru, w_ada, b_ada,
           norm_ffn1, ffn1_w_gate, ffn1_w_up, ffn1_w_down, norm_mix, w_in, pool_w, pool_scale,
           conv_w, conv_b, lru_wr, lru_br, lru_wi, lru_bi, lru_lambda, gmlp_norm, gmlp_ws, gmlp_bs,
           wbo_pool, wbo_lru, wbo_gmlp, w_out, norm_ffn2, ffn2_w_gate, ffn2_w_up, ffn2_w_down,
           norm_final):
    nbp, tp, _ = x_prompt.shape
    nbs, ts, _ = x_sample.shape
    tm_ffn, tm_mix, nb_s = 512, 256, 32

    mods_p, mods_s = _ada(c_prompt, c_sample, w_ada, b_ada)
    mods_p = mods_p.reshape(DEPTH, N_MOD, nbp, 1, D)
    mods_s = mods_s.reshape(DEPTH, N_MOD, nbs, 1, D)

    def row(v):
        return v.reshape(DEPTH, 1, v.shape[-1])

    bf = lambda w: w.astype(BF16)
    f1 = (row(norm_ffn1), bf(ffn1_w_gate), bf(ffn1_w_up), bf(ffn1_w_down))
    f2 = (row(norm_ffn2), bf(ffn2_w_gate), bf(ffn2_w_up), bf(ffn2_w_down))
    wri = bf(jnp.concatenate([lru_wr, lru_wi], axis=-1))
    mixw = (row(norm_mix), bf(w_in), bf(pool_w), row(pool_scale), conv_w, row(conv_b), wri,
            row(lru_br), row(lru_bi), row(lru_lambda), row(gmlp_norm),
            bf(wbo_pool), bf(wbo_lru), bf(wbo_gmlp), bf(w_out))
    gbs_rows = jnp.repeat(jnp.swapaxes(gmlp_bs, 1, 2), GMLP_GW, axis=2)
    tperm = np.array([(r % SUBLANES) * SEG + r // SUBLANES for r in range(CHUNK)])
    gws_p = gmlp_ws[:, :, tperm][:, :, :, tperm]
    gbs_rows_p = gbs_rows[:, tperm, :]
    nf = norm_final.reshape(1, D)
    sh0 = state_lru.reshape(DEPTH, nbs, 1, D)

    xp = x_prompt.reshape(nbp, tp // CHUNK, SUBLANES, SEG, D).swapaxes(2, 3).reshape(nbp, tp, D)
    xs = x_sample
    pp, pc, ph, sp, sc, sh, sv = [], [], [], [], [], [], []
    for l in range(DEPTH):
        final = l == DEPTH - 1
        xp = _ffn(xp, mods_p, l, 0, *f1, nf, nb=1, t=tm_ffn, final=False)
        xp, tailp, tailc, nh = _mix_prompt(xp, mods_p, l, mixw, gws_p, gbs_rows_p, tm=tm_mix)
        npool = tailp[:, SUBLANES - 1::SUBLANES][:, SEG - POOL_PREFIX:]
        nconv = tailc[:, SUBLANES - 1::SUBLANES][:, SEG - (CONV_W - 1):]
        xp = _ffn(xp, mods_p, l, 6, *f2, nf, nb=1, t=tm_ffn, final=final)
        pp.append(npool); pc.append(nconv); ph.append(nh.reshape(nbp, D))

        xs = _ffn(xs, mods_s, l, 0, *f1, nf, nb=nb_s, t=ts, final=False)
        xs, npool, nconv, nh, nv = _mix_sample(xs, mods_s, l, mixw, gmlp_ws, gbs_rows,
                                               state_pool, state_conv, sh0, nb=nb_s)
        xs = _ffn(xs, mods_s, l, 6, *f2, nf, nb=nb_s, t=ts, final=final)
        sp.append(npool); sc.append(nconv); sh.append(nh.reshape(nbs, D)); sv.append(nv)
    xp = xp.reshape(nbp, tp // CHUNK, SEG, SUBLANES, D).swapaxes(2, 3).reshape(nbp, tp, D)
    return (xp, xs, jnp.stack(pp), jnp.stack(pc), jnp.stack(ph),
            jnp.stack(sp), jnp.stack(sc), jnp.stack(sh), jnp.stack(sv))
```

```python
import functools

import numpy as np
import jax
import jax.numpy as jnp
from jax import lax
from jax.experimental import pallas as pl
from jax.experimental.pallas import tpu as pltpu

D = 1024
DEPTH = 4
PAST_LEN = 16384
POOL_WINDOWS = (2, 4, 8, 16)
POOL_GW = 256
POOL_PREFIX = 15
LRU_HEADS = 8
LRU_HD = 128
CONV_W = 4
LRU_C = 8.0
GMLP_W = 512
GMLP_GW = 128
CHUNK = 128
IN_W = 7168
D_FF = 2816
N_MOD = 9
EPS = 1e-6

GATE_COL0 = 3 * D + 2 * GMLP_W
LOG2E = 1.4426950408889634

FF_CHUNK = 256
SUBLANES = 8
LANES = 128
SEG = CHUNK // SUBLANES
CPREF = 8
VMEM_LIMIT = 56 * 1024 * 1024

F32 = jnp.float32
BF16 = jnp.bfloat16


def _dot(a, b):
    return jnp.dot(a, b, preferred_element_type=F32)


def _norm_mod(x, n, sc, sh):
    ms = jnp.mean(x * x, axis=-1, keepdims=True)
    y = x * lax.rsqrt(ms + EPS) * n
    return y * (1.0 + sc) + sh


def _ada_body(cp_ref, cs_ref, w_ref, b_ref, op_ref, os_ref):
    w = w_ref[...].astype(BF16)
    b = b_ref[...]

    def f(c):
        s = c * jax.nn.sigmoid(c)
        return _dot(s.astype(BF16), w) + b

    op_ref[...] = f(cp_ref[...])
    os_ref[...] = f(cs_ref[...])


def _ada(c_prompt, c_sample, w_ada, b_ada):
    nbp, nbs = c_prompt.shape[0], c_sample.shape[0]
    return pl.pallas_call(
        _ada_body,
        grid=(DEPTH, N_MOD),
        in_specs=[
            pl.BlockSpec((nbp, D), lambda l, j: (0, 0)),
            pl.BlockSpec((nbs, D), lambda l, j: (0, 0)),
            pl.BlockSpec((None, D, D), lambda l, j: (l, 0, j)),
            pl.BlockSpec((None, 1, D), lambda l, j: (l, 0, j)),
        ],
        out_specs=[
            pl.BlockSpec((None, None, nbp, D), lambda l, j: (l, j, 0, 0)),
            pl.BlockSpec((None, None, nbs, D), lambda l, j: (l, j, 0, 0)),
        ],
        out_shape=[
            jax.ShapeDtypeStruct((DEPTH, N_MOD, nbp, D), F32),
            jax.ShapeDtypeStruct((DEPTH, N_MOD, nbs, D), F32),
        ],
        name="ada",
    )(c_prompt, c_sample, w_ada, b_ada.reshape(DEPTH, 1, N_MOD * D))


def _ffn_body(x_ref, sh_ref, sc_ref, gt_ref, n_ref, wg_ref, wu_ref, wd_ref, nf_ref, o_ref, *, final):
    x = x_ref[...]
    nb, t, _ = x.shape
    rows = nb * t
    h = _norm_mod(x, n_ref[...], sc_ref[...], sh_ref[...])
    hb = h.reshape(rows, D).astype(BF16)
    acc = jnp.zeros((rows, D), F32)
    for c in range(D_FF // FF_CHUNK):
        sl = slice(c * FF_CHUNK, (c + 1) * FF_CHUNK)
        g = _dot(hb, wg_ref[:, sl])
        u = _dot(hb, wu_ref[:, sl])
        a = (g * jax.nn.sigmoid(g)) * u
        acc = acc + _dot(a.astype(BF16), wd_ref[sl, :])
    out = x + (0.5 * gt_ref[...]) * acc.reshape(nb, t, D)
    if final:
        ms = jnp.mean(out * out, axis=-1, keepdims=True)
        out = out * lax.rsqrt(ms + EPS) * nf_ref[...]
    o_ref[...] = out


def _mod_spec(l, k, nb):
    return pl.BlockSpec((None, None, nb, 1, D), lambda i, j: (l, k, i, 0, 0))


def _row_spec(l, width):
    return pl.BlockSpec((None, 1, width), lambda i, j: (l, 0, 0))


def _full_spec(l, shape):
    nd = len(shape)
    return pl.BlockSpec((None,) + tuple(shape), lambda i, j: (l,) + (0,) * nd,
                        pipeline_mode=pl.Buffered(1))


def _ffn(x, mods, l, k0, norm, wg, wu, wd, norm_final, *, nb, t, final):
    nbt, tt, _ = x.shape
    grid = (nbt // nb, tt // t)
    xspec = pl.BlockSpec((nb, t, D), lambda i, j: (i, j, 0))
    return pl.pallas_call(
        functools.partial(_ffn_body, final=final),
        grid=grid,
        in_specs=[
            xspec,
            _mod_spec(l, k0, nb), _mod_spec(l, k0 + 1, nb), _mod_spec(l, k0 + 2, nb),
            _row_spec(l, D),
            _full_spec(l, (D, D_FF)), _full_spec(l, (D, D_FF)), _full_spec(l, (D_FF, D)),
            pl.BlockSpec((1, D), lambda i, j: (0, 0)),
        ],
        out_specs=xspec,
        out_shape=jax.ShapeDtypeStruct(x.shape, F32),
        compiler_params=pltpu.CompilerParams(
            dimension_semantics=("arbitrary", "arbitrary"), vmem_limit_bytes=VMEM_LIMIT),
        name="ffn_final" if final else "ffn",
    )(x, mods, mods, mods, norm, wg, wu, wd, norm_final)


def _scan8(a, b):
    row = lax.broadcasted_iota(jnp.int32, a.shape, 0) & (SUBLANES - 1)
    for d in (1, 2, 4):
        m = row >= d
        a_s = jnp.where(m, pltpu.roll(a, d, 0), 1.0)
        b_s = jnp.where(m, pltpu.roll(b, d, 0), 0.0)
        b = a * b_s + b
        a = a * a_s
    return a, b


def _lru_coeffs(xc, wri_ref, br, bi, lam):
    xcb = xc.astype(BF16)
    rs, is_ = [], []
    for hd in range(LRU_HEADS):
        res = _dot(xcb[:, hd * LRU_HD:(hd + 1) * LRU_HD], wri_ref[hd])
        rs.append(res[:, :LRU_HD])
        is_.append(res[:, LRU_HD:])
    tr = jnp.tanh(jnp.concatenate(rs, axis=1) + 0.5 * br)
    ti = jnp.tanh(jnp.concatenate(is_, axis=1) + 0.5 * bi)
    nl = -lam
    softplus = jnp.maximum(nl, 0.0) + jnp.log1p(jnp.exp(-jnp.abs(nl)))
    c1 = (-0.5 * LRU_C * LOG2E) * softplus
    a = jnp.exp2(c1 * (tr + 1.0))
    y = 1.0 - a * a
    root = jnp.where(y > 0.0, y * lax.rsqrt(y), 0.0)
    bx = root * ((0.5 * ti + 0.5) * xc)
    return a, bx


def _pool_project(d_parts, poolw_ref, pscale):
    ys = [_dot(d.astype(BF16), poolw_ref[g]) for g, d in enumerate(d_parts)]
    return jnp.concatenate(ys, axis=1) * pscale


def _rms(v, g):
    ms = jnp.mean(v * v, axis=-1, keepdims=True)
    return v * lax.rsqrt(ms + EPS) * g


def _merge_out(hb, win_ref, y_pool, y_lru, y_gm, wbop_ref, wbol_ref, wbog_ref, wout_ref):
    def gate(k):
        c0 = GATE_COL0 + k * D
        return jnp.tanh(_dot(hb, win_ref[:, c0:c0 + D])) + 1.0

    merged = gate(0) * _dot(y_pool.astype(BF16), wbop_ref[...])
    merged = merged + gate(1) * _dot(y_lru.astype(BF16), wbol_ref[...])
    merged = merged + gate(2) * _dot(y_gm.astype(BF16), wbog_ref[...])
    return _dot(merged.astype(BF16), wout_ref[...])


def _extend(x, carry_ref, e_ref):
    tm, cdim = x.shape
    nc = tm // CHUNK
    xr = pltpu.roll(x.reshape(tm // SUBLANES, SUBLANES, cdim), 1, 1).reshape(tm, cdim)
    prev = carry_ref[...]
    if nc > 1:
        prev = jnp.concatenate([prev, xr[:tm - CHUNK]], axis=0)
    sub0 = (lax.broadcasted_iota(jnp.int32, (tm, cdim), 0) & (SUBLANES - 1)) == 0
    e_ref[:, 0:CHUNK, :] = jnp.where(sub0, prev, xr).reshape(nc, CHUNK, cdim)
    e_ref[:, CHUNK:2 * CHUNK, :] = x.reshape(nc, CHUNK, cdim)
    carry_ref[...] = xr[tm - CHUNK:]


def _shifted(e_ref, k, cols=slice(None)):
    v = e_ref[:, pl.ds((SEG - k) * SUBLANES, CHUNK), cols]
    return v.reshape(v.shape[0] * CHUNK, v.shape[2])


def _window_sum(e_ref, ta_ref, tb_ref, c0, w):
    src, cols = e_ref, slice(c0, c0 + POOL_GW)
    bufs = (ta_ref, tb_ref)
    k, level, lo = 1, 0, 0
    while True:
        last = 2 * k == w
        lo = lo + k
        start = SEG if last else lo
        n = (2 * SEG - start) * SUBLANES
        val = (src[:, pl.ds(start * SUBLANES, n), cols]
               + src[:, pl.ds((start - k) * SUBLANES, n), cols])
        if last:
            return val.reshape(val.shape[0] * CHUNK, POOL_GW)
        dst = bufs[level % 2]
        dst[:, pl.ds(start * SUBLANES, n), :] = val
        src, cols = dst, slice(0, POOL_GW)
        k, level = 2 * k, level + 1


def _stored_time(shape, axis):
    r = lax.broadcasted_iota(jnp.int32, shape, axis)
    return (r & (SUBLANES - 1)) * SEG + ((r >> 3) & (SEG - 1))


def _mix_prompt_body(x_ref, sh_ref, sc_ref, gt_ref, nm_ref, win_ref, poolw_ref, pscale_ref,
                     convw_ref, convb_ref, wri_ref, br_ref, bi_ref, lam_ref, gnorm_ref, gws_ref,
                     gbs_ref, wbop_ref, wbol_ref, wbog_ref, wout_ref,
                     o_ref, tailp_ref, tailc_ref, nh_ref,
                     ep_ref, cp_ref, ta_ref, tb_ref, ec_ref, cc_ref, pa_ref, h_ref, hs_ref, hc_ref,
                     yg_ref, *, tm):
    j = pl.program_id(1)
    nt = pl.num_programs(1)
    nc = tm // CHUNK

    @pl.when(j == 0)
    def _():
        cp_ref[...] = jnp.zeros((CHUNK, D), F32)
        cc_ref[...] = jnp.zeros((CHUNK, D), F32)
        hc_ref[...] = jnp.zeros((1, D), F32)

    x = x_ref[...]
    h = _norm_mod(x, nm_ref[...], sc_ref[...], sh_ref[...])
    hb = h.reshape(tm, D).astype(BF16)

    xp = _dot(hb, win_ref[:, 0:D])
    _extend(xp, cp_ref, ep_ref)
    row = lax.broadcasted_iota(jnp.int32, (tm, LANES), 0)
    pos = j * tm + (row & ~(CHUNK - 1)) + _stored_time((tm, LANES), 0)
    d_parts = []
    for g, w in enumerate(POOL_WINDOWS):
        s = _window_sum(ep_ref, ta_ref, tb_ref, g * POOL_GW, w)
        inv = 1.0 / jnp.minimum(w, pos + 1).astype(F32)
        inv = jnp.concatenate([inv] * (POOL_GW // LANES), axis=1)
        d_parts.append(s * inv - xp[:, g * POOL_GW:(g + 1) * POOL_GW])
    y_pool = _pool_project(d_parts, poolw_ref, pscale_ref[...])

    xl = _dot(hb, win_ref[:, D:2 * D])
    _extend(xl, cc_ref, ec_ref)
    cw = convw_ref[...]
    xc = convb_ref[...] + cw[CONV_W - 1:CONV_W, :] * xl
    for k in range(CONV_W - 1):
        xc = xc + cw[k:k + 1, :] * _shifted(ec_ref, CONV_W - 1 - k)

    @pl.when(j == nt - 1)
    def _():
        tailp_ref[...] = xp[tm - CHUNK:, :]
        tailc_ref[...] = xl[tm - CHUNK:, :]

    a, bx = _lru_coeffs(xc, wri_ref, br_ref[...], bi_ref[...], lam_ref[...])
    a3 = a.reshape(nc, CHUNK, D)
    b3 = bx.reshape(nc, CHUNK, D)
    hq = b3[:, 0:SUBLANES, :]
    paq = a3[:, 0:SUBLANES, :]
    pa_ref[:, 0:SUBLANES, :] = paq
    h_ref[:, 0:SUBLANES, :] = hq
    for q in range(1, SEG):
        rows = slice(q * SUBLANES, (q + 1) * SUBLANES)
        hq = a3[:, rows, :] * hq + b3[:, rows, :]
        paq = a3[:, rows, :] * paq
        pa_ref[:, rows, :] = paq
        h_ref[:, rows, :] = hq
    g_state = hc_ref[...]
    entry = []
    for c in range(nc):
        rows_c = []
        for s in range(SUBLANES):
            rows_c.append(g_state)
            g_state = paq[c, s:s + 1, :] * g_state + hq[c, s:s + 1, :]
        entry.append(jnp.concatenate(rows_c, axis=0))
    entry = jnp.stack(entry, axis=0)
    hc_ref[...] = g_state

    @pl.when(j == nt - 1)
    def _():
        nh_ref[...] = g_state

    for q in range(SEG):
        rows = slice(q * SUBLANES, (q + 1) * SUBLANES)
        hs_ref[:, rows, :] = pa_ref[:, rows, :] * entry + h_ref[:, rows, :]

    gl = _dot(hb, win_ref[:, 2 * D:3 * D])
    y_lru = hs_ref[...].reshape(tm, D) * jax.nn.gelu(gl)

    u = _dot(hb, win_ref[:, 3 * D:3 * D + GMLP_W])
    v = _dot(hb, win_ref[:, 3 * D + GMLP_W:4 * D])
    vnb = _rms(v, gnorm_ref[...]).astype(BF16)
    tri = _stored_time((CHUNK, CHUNK), 0) >= _stored_time((CHUNK, CHUNK), 1)
    gbs = gbs_ref[...]
    for g in range(GMLP_W // GMLP_GW):
        wm = jnp.where(tri, gws_ref[g], 0.0).astype(BF16)
        cols = slice(g * GMLP_GW, (g + 1) * GMLP_GW)
        for c in range(nc):
            rows = slice(c * CHUNK, (c + 1) * CHUNK)
            s = _dot(wm, vnb[rows, cols]) + gbs[:, cols]
            yg_ref[rows, cols] = u[rows, cols] * s
    y_gm = yg_ref[...]

    y = _merge_out(hb, win_ref, y_pool, y_lru, y_gm, wbop_ref, wbol_ref, wbog_ref, wout_ref)
    o_ref[...] = x + gt_ref[...] * y.reshape(1, tm, D)


def _mix_weight_specs(l):
    return [
        _row_spec(l, D),
        _full_spec(l, (D, IN_W)),
        _full_spec(l, (len(POOL_WINDOWS), POOL_GW, POOL_GW)),
        _row_spec(l, D),
        pl.BlockSpec((None, CONV_W, D), lambda i, j: (l, 0, 0)),
        _row_spec(l, D),
        _full_spec(l, (LRU_HEADS, LRU_HD, 2 * LRU_HD)),
        _row_spec(l, D), _row_spec(l, D), _row_spec(l, D),
        _row_spec(l, GMLP_W),
    ]


def _mix_proj_specs(l):
    return [_full_spec(l, (D, D)), _full_spec(l, (D, D)), _full_spec(l, (GMLP_W, D)),
            _full_spec(l, (D, D))]


def _mix_prompt(x, mods, l, wts, gws, gbs_rows, *, tm):
    nbt, tt, _ = x.shape
    grid = (nbt, tt // tm)
    xspec = pl.BlockSpec((1, tm, D), lambda i, j: (i, j, 0))
    in_specs = ([xspec, _mod_spec(l, 3, 1), _mod_spec(l, 4, 1), _mod_spec(l, 5, 1)]
                + _mix_weight_specs(l)
                + [pl.BlockSpec((None, GMLP_W // GMLP_GW, CHUNK, CHUNK), lambda i, j: (l, 0, 0, 0)),
                   pl.BlockSpec((None, CHUNK, GMLP_W), lambda i, j: (l, 0, 0))]
                + _mix_proj_specs(l))
    nc = tm // CHUNK
    out_specs = [
        xspec,
        pl.BlockSpec((None, CHUNK, D), lambda i, j: (i, 0, 0)),
        pl.BlockSpec((None, CHUNK, D), lambda i, j: (i, 0, 0)),
        pl.BlockSpec((None, 1, D), lambda i, j: (i, 0, 0)),
    ]
    out_shape = [
        jax.ShapeDtypeStruct(x.shape, F32),
        jax.ShapeDtypeStruct((nbt, CHUNK, D), F32),
        jax.ShapeDtypeStruct((nbt, CHUNK, D), F32),
        jax.ShapeDtypeStruct((nbt, 1, D), F32),
    ]
    scratch = [
        pltpu.VMEM((nc, 2 * CHUNK, D), F32),
        pltpu.VMEM((CHUNK, D), F32),
        pltpu.VMEM((nc, 2 * CHUNK, POOL_GW), F32),
        pltpu.VMEM((nc, 2 * CHUNK, POOL_GW), F32),
        pltpu.VMEM((nc, 2 * CHUNK, D), F32),
        pltpu.VMEM((CHUNK, D), F32),
        pltpu.VMEM((nc, CHUNK, D), F32),
        pltpu.VMEM((nc, CHUNK, D), F32),
        pltpu.VMEM((nc, CHUNK, D), F32),
        pltpu.VMEM((1, D), F32),
        pltpu.VMEM((tm, GMLP_W), F32),
    ]
    (norm, w_in, pool_w, pool_scale, conv_w, conv_b, wri, br, bi, lam, gnorm,
     wbop, wbol, wbog, wout) = wts
    return pl.pallas_call(
        functools.partial(_mix_prompt_body, tm=tm),
        grid=grid,
        in_specs=in_specs,
        out_specs=out_specs,
        out_shape=out_shape,
        scratch_shapes=scratch,
        compiler_params=pltpu.CompilerParams(
            dimension_semantics=("arbitrary", "arbitrary"), vmem_limit_bytes=VMEM_LIMIT),
        name="mix_prompt",
    )(x, mods, mods, mods, norm, w_in, pool_w, pool_scale, conv_w, conv_b, wri, br, bi, lam, gnorm,
      gws, gbs_rows, wbop, wbol, wbog, wout)


def _mix_sample_body(x_ref, sh_ref, sc_ref, gt_ref, nm_ref, win_ref, poolw_ref, pscale_ref,
                     convw_ref, convb_ref, wri_ref, br_ref, bi_ref, lam_ref, gnorm_ref, gws_ref,
                     gbs_ref, wbop_ref, wbol_ref, wbog_ref, wout_ref,
                     spool_ref, sconv_ref, sh0_ref,
                     o_ref, npool_ref, nconv_ref, nh_ref, vn_ref,
                     p_ref, c_ref, *, nb, t):
    rows = nb * t
    x = x_ref[...]
    h = _norm_mod(x, nm_ref[...], sc_ref[...], sh_ref[...])
    hb = h.reshape(rows, D).astype(BF16)

    xp = _dot(hb, win_ref[:, 0:D]).reshape(nb, t, D)
    p_ref[:, pl.ds(1, POOL_PREFIX), :] = spool_ref[...]
    p_ref[:, pl.ds(POOL_PREFIX + 1, t), :] = xp
    d_parts = []
    for g, w in enumerate(POOL_WINDOWS):
        cols = slice(g * POOL_GW, (g + 1) * POOL_GW)
        s = xp[:, :, cols]
        for k in range(1, w):
            s = s + p_ref[:, pl.ds(POOL_PREFIX + 1 - k, t), cols]
        d_parts.append((s / float(w) - xp[:, :, cols]).reshape(rows, POOL_GW))
    y_pool = _pool_project(d_parts, poolw_ref, pscale_ref[...])
    npool_ref[...] = p_ref[:, pl.ds(t + 1, POOL_PREFIX), :]

    xl = _dot(hb, win_ref[:, D:2 * D]).reshape(nb, t, D)
    c_ref[:, pl.ds(CPREF - (CONV_W - 1), CONV_W - 1), :] = sconv_ref[...]
    c_ref[:, pl.ds(CPREF, t), :] = xl
    cw = convw_ref[...]
    xc = convb_ref[...] + cw[3:4, :] * xl
    for k in range(CONV_W - 1):
        xc = xc + cw[k:k + 1, :] * c_ref[:, pl.ds(CPREF - (CONV_W - 1) + k, t), :]
    nconv_ref[...] = c_ref[:, pl.ds(CPREF + t - (CONV_W - 1), CONV_W - 1), :]

    a, bx = _lru_coeffs(xc.reshape(rows, D), wri_ref, br_ref[...], bi_ref[...], lam_ref[...])
    a, bx = _scan8(a, bx)
    hs = a.reshape(nb, t, D) * sh0_ref[...] + bx.reshape(nb, t, D)
    nh_ref[...] = hs[:, t - 1:t, :]
    gl = _dot(hb, win_ref[:, 2 * D:3 * D])
    y_lru = hs.reshape(rows, D) * jax.nn.gelu(gl)

    u = _dot(hb, win_ref[:, 3 * D:3 * D + GMLP_W])
    v = _dot(hb, win_ref[:, 3 * D + GMLP_W:4 * D])
    vn = _rms(v, gnorm_ref[...]).reshape(nb, t, GMLP_W)
    vn_ref[...] = vn
    tri = (lax.broadcasted_iota(jnp.int32, (t, t), 0) >= lax.broadcasted_iota(jnp.int32, (t, t), 1))
    gbs = gbs_ref[...]
    s_parts = []
    for g in range(GMLP_W // GMLP_GW):
        cols = slice(g * GMLP_GW, (g + 1) * GMLP_GW)
        wm = jnp.where(tri, gws_ref[g][0:t, 0:t], 0.0)
        s = jnp.broadcast_to(gbs[0:t, cols], (nb, t, GMLP_GW))
        for k in range(t):
            s = s + wm[:, k:k + 1] * vn[:, k:k + 1, cols]
        s_parts.append(s)
    y_gm = u * jnp.concatenate(s_parts, axis=2).reshape(rows, GMLP_W)

    y = _merge_out(hb, win_ref, y_pool, y_lru, y_gm, wbop_ref, wbol_ref, wbog_ref, wout_ref)
    o_ref[...] = x + gt_ref[...] * y.reshape(nb, t, D)


def _mix_sample(x, mods, l, wts, gws, gbs_rows, spool, sconv, sh0, *, nb):
    nbt, t, _ = x.shape
    grid = (nbt // nb, 1)
    xspec = pl.BlockSpec((nb, t, D), lambda i, j: (i, 0, 0))

    def state_spec(r):
        return pl.BlockSpec((None, nb, r, D), lambda i, j: (l, i, 0, 0))

    def out_state_spec(r):
        return pl.BlockSpec((nb, r, D), lambda i, j: (i, 0, 0))

    in_specs = ([xspec, _mod_spec(l, 3, nb), _mod_spec(l, 4, nb), _mod_spec(l, 5, nb)]
                + _mix_weight_specs(l)
                + [pl.BlockSpec((None, GMLP_W // GMLP_GW, CHUNK, CHUNK), lambda i, j: (l, 0, 0, 0)),
                   pl.BlockSpec((None, CHUNK, GMLP_W), lambda i, j: (l, 0, 0))]
                + _mix_proj_specs(l)
                + [state_spec(POOL_PREFIX), state_spec(CONV_W - 1), state_spec(1)])
    out_specs = [xspec, out_state_spec(POOL_PREFIX), out_state_spec(CONV_W - 1), out_state_spec(1),
                 pl.BlockSpec((nb, t, GMLP_W), lambda i, j: (i, 0, 0))]
    out_shape = [
        jax.ShapeDtypeStruct(x.shape, F32),
        jax.ShapeDtypeStruct((nbt, POOL_PREFIX, D), F32),
        jax.ShapeDtypeStruct((nbt, CONV_W - 1, D), F32),
        jax.ShapeDtypeStruct((nbt, 1, D), F32),
        jax.ShapeDtypeStruct((nbt, t, GMLP_W), F32),
    ]
    scratch = [
        pltpu.VMEM((nb, POOL_PREFIX + 1 + t, D), F32),
        pltpu.VMEM((nb, CPREF + t, D), F32),
    ]
    (norm, w_in, pool_w, pool_scale, conv_w, conv_b, wri, br, bi, lam, gnorm,
     wbop, wbol, wbog, wout) = wts
    return pl.pallas_call(
        functools.partial(_mix_sample_body, nb=nb, t=t),
        grid=grid,
        in_specs=in_specs,
        out_specs=out_specs,
        out_shape=out_shape,
        scratch_shapes=scratch,
        compiler_params=pltpu.CompilerParams(
            dimension_semantics=("arbitrary", "arbitrary"), vmem_limit_bytes=VMEM_LIMIT),
        name="mix_sample",
    )(x, mods, mods, mods, norm, w_in, pool_w, pool_scale, conv_w, conv_b, wri, br, bi, lam, gnorm,
      gws, gbs_rows, wbop, wbol, wbog, wout, spool, sconv, sh0)


def kernel(x_prompt, x_sample, c_prompt, c_sample, state_pool, state_conv, state_lru, w_ada, b_ada,
           norm_ffn1, ffn1_w_gate, ffn1_w_up, ffn1_w_down, norm_mix, w_in, pool_w, pool_scale,
           conv_w, conv_b, lru_wr, lru_br, lru_wi, lru_bi, lru_lambda, gmlp_norm, gmlp_ws, gmlp_bs,
           wbo_pool, wbo_lru, wbo_gmlp, w_out, norm_ffn2, ffn2_w_gate, ffn2_w_up, ffn2_w_down,
           norm_final):
    nbp, tp, _ = x_prompt.shape
    nbs, ts, _ = x_sample.shape
    tm_ffn, tm_mix, nb_s = 1024, 512, 32

    mods_p, mods_s = _ada(c_prompt, c_sample, w_ada, b_ada)
    mods_p = mods_p.reshape(DEPTH, N_MOD, nbp, 1, D)
    mods_s = mods_s.reshape(DEPTH, N_MOD, nbs, 1, D)

    def row(v):
        return v.reshape(DEPTH, 1, v.shape[-1])

    bf = lambda w: w.astype(BF16)
    f1 = (row(norm_ffn1), bf(ffn1_w_gate), bf(ffn1_w_up), bf(ffn1_w_down))
    f2 = (row(norm_ffn2), bf(ffn2_w_gate), bf(ffn2_w_up), bf(ffn2_w_down))
    wri = bf(0.5 * jnp.concatenate([lru_wr, lru_wi], axis=-1))
    col_scale = jnp.where(jnp.arange(IN_W) >= GATE_COL0, 0.5, 1.0).astype(F32)
    mixw = (row(norm_mix), bf(w_in * col_scale), bf(pool_w), row(pool_scale), conv_w, row(conv_b),
            wri, row(lru_br), row(lru_bi), row(lru_lambda), row(gmlp_norm),
            bf(0.5 * wbo_pool), bf(0.5 * wbo_lru), bf(0.5 * wbo_gmlp), bf(w_out))
    gbs_rows = jnp.repeat(jnp.swapaxes(gmlp_bs, 1, 2), GMLP_GW, axis=2)
    tperm = np.array([(r % SUBLANES) * SEG + r // SUBLANES for r in range(CHUNK)])
    gws_p = gmlp_ws[:, :, tperm][:, :, :, tperm]
    gbs_rows_p = gbs_rows[:, tperm, :]
    nf = norm_final.reshape(1, D)
    sh0 = state_lru.reshape(DEPTH, nbs, 1, D)

    xp = x_prompt.reshape(nbp, tp // CHUNK, SUBLANES, SEG, D).swapaxes(2, 3).reshape(nbp, tp, D)
    xs = x_sample
    pp, pc, ph, sp, sc, sh, sv = [], [], [], [], [], [], []
    for l in range(DEPTH):
        final = l == DEPTH - 1
        xp = _ffn(xp, mods_p, l, 0, *f1, nf, nb=1, t=tm_ffn, final=False)
        xp, tailp, tailc, nh = _mix_prompt(xp, mods_p, l, mixw, gws_p, gbs_rows_p, tm=tm_mix)
        npool = tailp[:, SUBLANES - 1::SUBLANES][:, SEG - POOL_PREFIX:]
        nconv = tailc[:, SUBLANES - 1::SUBLANES][:, SEG - (CONV_W - 1):]
        xp = _ffn(xp, mods_p, l, 6, *f2, nf, nb=1, t=tm_ffn, final=final)
        pp.append(npool); pc.append(nconv); ph.append(nh.reshape(nbp, D))

        xs = _ffn(xs, mods_s, l, 0, *f1, nf, nb=nb_s, t=ts, final=False)
        xs, npool, nconv, nh, nv = _mix_sample(xs, mods_s, l, mixw, gmlp_ws, gbs_rows,
                                               state_pool, state_conv, sh0, nb=nb_s)
        xs = _ffn(xs, mods_s, l, 6, *f2, nf, nb=nb_s, t=ts, final=final)
        sp.append(npool); sc.append(nconv); sh.append(nh.reshape(nbs, D)); sv.append(nv)
    xp = xp.reshape(nbp, tp // CHUNK, SEG, SUBLANES, D).swapaxes(2, 3).reshape(nbp, tp, D)
    return (xp, xs, jnp.stack(pp), jnp.stack(pc), jnp.stack(ph),
            jnp.stack(sp), jnp.stack(sc), jnp.stack(sh), jnp.stack(sv))
```

```python
import functools

import numpy as np
import jax
import jax.numpy as jnp
from jax import lax
from jax.experimental import pallas as pl
from jax.experimental.pallas import tpu as pltpu

D = 1024
DEPTH = 4
PAST_LEN = 16384
POOL_WINDOWS = (2, 4, 8, 16)
POOL_GW = 256
POOL_PREFIX = 15
LRU_HEADS = 8
LRU_HD = 128
CONV_W = 4
LRU_C = 8.0
GMLP_W = 512
GMLP_GW = 128
CHUNK = 128
IN_W = 7168
D_FF = 2816
N_MOD = 9
EPS = 1e-6

GATE_COL0 = 3 * D + 2 * GMLP_W
LOG2E = 1.4426950408889634

FF_CHUNK = 256
N_FF_CHUNKS = D_FF // FF_CHUNK
SUBLANES = 8
LANES = 128
SEG = CHUNK // SUBLANES
CPREF = 8
VMEM_LIMIT = 56 * 1024 * 1024

F32 = jnp.float32
BF16 = jnp.bfloat16


def _dot(a, b):
    return jnp.dot(a, b, preferred_element_type=F32)


def _norm_mod(x, n, sc, sh):
    ms = jnp.mean(x * x, axis=-1, keepdims=True)
    y = x * lax.rsqrt(ms + EPS) * n
    return y * (1.0 + sc) + sh


def _ada_body(cp_ref, cs_ref, w_ref, b_ref, op_ref, os_ref):
    w = w_ref[...].astype(BF16)
    b = b_ref[...]

    def f(c):
        s = c * jax.nn.sigmoid(c)
        return _dot(s.astype(BF16), w) + b

    op_ref[...] = f(cp_ref[...])[:, None, :]
    os_ref[...] = f(cs_ref[...])[:, None, :]


def _ada(c_prompt, c_sample, w_ada, b_ada):
    nbp, nbs = c_prompt.shape[0], c_sample.shape[0]
    return pl.pallas_call(
        _ada_body,
        grid=(DEPTH, N_MOD),
        in_specs=[
            pl.BlockSpec((nbp, D), lambda l, j: (0, 0)),
            pl.BlockSpec((nbs, D), lambda l, j: (0, 0)),
            pl.BlockSpec((None, D, D), lambda l, j: (l, 0, j)),
            pl.BlockSpec((None, 1, D), lambda l, j: (l, 0, j)),
        ],
        out_specs=[
            pl.BlockSpec((None, None, nbp, 1, D), lambda l, j: (l, j, 0, 0, 0)),
            pl.BlockSpec((None, None, nbs, 1, D), lambda l, j: (l, j, 0, 0, 0)),
        ],
        out_shape=[
            jax.ShapeDtypeStruct((DEPTH, N_MOD, nbp, 1, D), F32),
            jax.ShapeDtypeStruct((DEPTH, N_MOD, nbs, 1, D), F32),
        ],
        name="ada",
    )(c_prompt, c_sample, w_ada, b_ada.reshape(DEPTH, 1, N_MOD * D))


def _ffn_tile(x_ref, sh_ref, sc_ref, gt_ref, n_ref, wg_s, wu_s, wd_s, nf_ref, o_ref, final):
    x = x_ref[...]
    nb, t, _ = x.shape
    rows = nb * t
    h = _norm_mod(x, n_ref[...], sc_ref[...], sh_ref[...])
    hb = h.reshape(rows, D).astype(BF16)
    acc = jnp.zeros((rows, D), F32)
    for c in range(N_FF_CHUNKS):
        g = _dot(hb, wg_s[c])
        u = _dot(hb, wu_s[c])
        a = (g * jax.nn.sigmoid(g)) * u
        acc = acc + _dot(a.astype(BF16), wd_s[c])
    out = x + (0.5 * gt_ref[...]) * acc.reshape(nb, t, D)
    if final:
        ms = jnp.mean(out * out, axis=-1, keepdims=True)
        out = out * lax.rsqrt(ms + EPS) * nf_ref[...]
    o_ref[...] = out


def _ffn_body(xp_ref, shp_ref, scp_ref, gtp_ref, xs_ref, shs_ref, scs_ref, gts_ref, n_ref,
              wg_ref, wu_ref, wd_ref, nf_ref, op_ref, os_ref, wg_s, wu_s, wd_s, *, n_p, final):
    i = pl.program_id(0)

    @pl.when(i < N_FF_CHUNKS)
    def _():
        wg_s[i] = wg_ref[...].astype(BF16)
        wu_s[i] = wu_ref[...].astype(BF16)
        wd_s[i] = wd_ref[...].astype(BF16)

    @pl.when(jnp.logical_and(i >= N_FF_CHUNKS, i < N_FF_CHUNKS + n_p))
    def _():
        _ffn_tile(xp_ref, shp_ref, scp_ref, gtp_ref, n_ref, wg_s, wu_s, wd_s, nf_ref, op_ref, final)

    @pl.when(i >= N_FF_CHUNKS + n_p)
    def _():
        _ffn_tile(xs_ref, shs_ref, scs_ref, gts_ref, n_ref, wg_s, wu_s, wd_s, nf_ref, os_ref, final)


def _ffn(xp, xs, mods_p, mods_s, l, k0, norm, wg, wu, wd, norm_final, *, tm, nb, final):
    nbp, tp, _ = xp.shape
    nbs, ts, _ = xs.shape
    tpb = tp // tm
    n_p, n_s = nbp * tpb, nbs // nb

    def p_tile(i):
        return jnp.clip(i - N_FF_CHUNKS, 0, n_p - 1)

    def s_tile(i):
        return jnp.clip(i - N_FF_CHUNKS - n_p, 0, n_s - 1)

    def w_chunk(i):
        return jnp.minimum(i, N_FF_CHUNKS - 1)

    def modp(k):
        return pl.BlockSpec((None, None, 1, 1, D), lambda i: (l, k, p_tile(i) // tpb, 0, 0))

    def mods(k):
        return pl.BlockSpec((None, None, nb, 1, D), lambda i: (l, k, s_tile(i), 0, 0))

    xp_spec = pl.BlockSpec((1, tm, D), lambda i: (p_tile(i) // tpb, p_tile(i) % tpb, 0))
    xs_spec = pl.BlockSpec((nb, ts, D), lambda i: (s_tile(i), 0, 0))
    w_col = pl.BlockSpec((None, D, FF_CHUNK), lambda i: (l, 0, w_chunk(i)))
    w_row = pl.BlockSpec((None, FF_CHUNK, D), lambda i: (l, w_chunk(i), 0))
    return pl.pallas_call(
        functools.partial(_ffn_body, n_p=n_p, final=final),
        grid=(N_FF_CHUNKS + n_p + n_s,),
        in_specs=[
            xp_spec, modp(k0), modp(k0 + 1), modp(k0 + 2),
            xs_spec, mods(k0), mods(k0 + 1), mods(k0 + 2),
            pl.BlockSpec((None, 1, D), lambda i: (l, 0, 0)),
            w_col, w_col, w_row,
            pl.BlockSpec((1, D), lambda i: (0, 0)),
        ],
        out_specs=[xp_spec, xs_spec],
        out_shape=[jax.ShapeDtypeStruct(xp.shape, F32), jax.ShapeDtypeStruct(xs.shape, F32)],
        scratch_shapes=[
            pltpu.VMEM((N_FF_CHUNKS, D, FF_CHUNK), BF16),
            pltpu.VMEM((N_FF_CHUNKS, D, FF_CHUNK), BF16),
            pltpu.VMEM((N_FF_CHUNKS, FF_CHUNK, D), BF16),
        ],
        compiler_params=pltpu.CompilerParams(
            dimension_semantics=("arbitrary",), vmem_limit_bytes=VMEM_LIMIT),
        name="ffn_final" if final else "ffn",
    )(xp, mods_p, mods_p, mods_p, xs, mods_s, mods_s, mods_s, norm, wg, wu, wd, norm_final)


def _mod_spec(l, k, nb):
    return pl.BlockSpec((None, None, nb, 1, D), lambda i, j: (l, k, i, 0, 0))


def _row_spec(l, width):
    return pl.BlockSpec((None, 1, width), lambda i, j: (l, 0, 0))


def _full_spec(l, shape):
    nd = len(shape)
    return pl.BlockSpec((None,) + tuple(shape), lambda i, j: (l,) + (0,) * nd,
                        pipeline_mode=pl.Buffered(1))


def _scan8(a, b):
    row = lax.broadcasted_iota(jnp.int32, a.shape, 0) & (SUBLANES - 1)
    for d in (1, 2, 4):
        m = row >= d
        a_s = jnp.where(m, pltpu.roll(a, d, 0), 1.0)
        b_s = jnp.where(m, pltpu.roll(b, d, 0), 0.0)
        b = a * b_s + b
        a = a * a_s
    return a, b


def _lru_coeffs(xc, wri_ref, br, bi, lam):
    xcb = xc.astype(BF16)
    rs, is_ = [], []
    for hd in range(LRU_HEADS):
        res = _dot(xcb[:, hd * LRU_HD:(hd + 1) * LRU_HD], wri_ref[hd])
        rs.append(res[:, :LRU_HD])
        is_.append(res[:, LRU_HD:])
    tr = jnp.tanh(jnp.concatenate(rs, axis=1) + 0.5 * br)
    ti = jnp.tanh(jnp.concatenate(is_, axis=1) + 0.5 * bi)
    nl = -lam
    softplus = jnp.maximum(nl, 0.0) + jnp.log1p(jnp.exp(-jnp.abs(nl)))
    c1 = (-0.5 * LRU_C * LOG2E) * softplus
    a = jnp.exp2(c1 * (tr + 1.0))
    y = 1.0 - a * a
    root = jnp.where(y > 0.0, y * lax.rsqrt(y), 0.0)
    bx = root * ((0.5 * ti + 0.5) * xc)
    return a, bx


def _pool_project(d_parts, poolw_ref, pscale):
    ys = [_dot(d.astype(BF16), poolw_ref[g]) for g, d in enumerate(d_parts)]
    return jnp.concatenate(ys, axis=1) * pscale


def _rms(v, g):
    ms = jnp.mean(v * v, axis=-1, keepdims=True)
    return v * lax.rsqrt(ms + EPS) * g


def _merge_out(hb, win_ref, y_pool, y_lru, y_gm, wbop_ref, wbol_ref, wbog_ref, wout_ref):
    def gate(k):
        c0 = GATE_COL0 + k * D
        return jnp.tanh(_dot(hb, win_ref[:, c0:c0 + D])) + 1.0

    merged = gate(0) * _dot(y_pool.astype(BF16), wbop_ref[...])
    merged = merged + gate(1) * _dot(y_lru.astype(BF16), wbol_ref[...])
    merged = merged + gate(2) * _dot(y_gm.astype(BF16), wbog_ref[...])
    return _dot(merged.astype(BF16), wout_ref[...])


def _extend(x, carry_ref, e_ref):
    tm, cdim = x.shape
    nc = tm // CHUNK
    xr = pltpu.roll(x.reshape(tm // SUBLANES, SUBLANES, cdim), 1, 1).reshape(tm, cdim)
    prev = carry_ref[...]
    if nc > 1:
        prev = jnp.concatenate([prev, xr[:tm - CHUNK]], axis=0)
    sub0 = (lax.broadcasted_iota(jnp.int32, (tm, cdim), 0) & (SUBLANES - 1)) == 0
    e_ref[:, 0:CHUNK, :] = jnp.where(sub0, prev, xr).reshape(nc, CHUNK, cdim)
    e_ref[:, CHUNK:2 * CHUNK, :] = x.reshape(nc, CHUNK, cdim)
    carry_ref[...] = xr[tm - CHUNK:]


def _shifted(e_ref, k, cols=slice(None)):
    v = e_ref[:, pl.ds((SEG - k) * SUBLANES, CHUNK), cols]
    return v.reshape(v.shape[0] * CHUNK, v.shape[2])


def _window_sum(e_ref, ta_ref, tb_ref, c0, w):
    src, cols = e_ref, slice(c0, c0 + POOL_GW)
    bufs = (ta_ref, tb_ref)
    k, level, lo = 1, 0, 0
    while True:
        last = 2 * k == w
        lo = lo + k
        start = SEG if last else lo
        n = (2 * SEG - start) * SUBLANES
        val = (src[:, pl.ds(start * SUBLANES, n), cols]
               + src[:, pl.ds((start - k) * SUBLANES, n), cols])
        if last:
            return val.reshape(val.shape[0] * CHUNK, POOL_GW)
        dst = bufs[level % 2]
        dst[:, pl.ds(start * SUBLANES, n), :] = val
        src, cols = dst, slice(0, POOL_GW)
        k, level = 2 * k, level + 1


def _stored_time(shape, axis):
    r = lax.broadcasted_iota(jnp.int32, shape, axis)
    return (r & (SUBLANES - 1)) * SEG + ((r >> 3) & (SEG - 1))


def _mix_prompt_body(x_ref, sh_ref, sc_ref, gt_ref, nm_ref, win_ref, poolw_ref, pscale_ref,
                     convw_ref, convb_ref, wri_ref, br_ref, bi_ref, lam_ref, gnorm_ref, gws_ref,
                     gbs_ref, wbop_ref, wbol_ref, wbog_ref, wout_ref,
                     o_ref, tailp_ref, tailc_ref, nh_ref,
                     ep_ref, cp_ref, ta_ref, tb_ref, ec_ref, cc_ref, pa_ref, h_ref, hs_ref, hc_ref,
                     yg_ref, *, tm):
    j = pl.program_id(1)
    nt = pl.num_programs(1)
    nc = tm // CHUNK

    @pl.when(j == 0)
    def _():
        cp_ref[...] = jnp.zeros((CHUNK, D), F32)
        cc_ref[...] = jnp.zeros((CHUNK, D), F32)
        hc_ref[...] = jnp.zeros((1, D), F32)

    x = x_ref[...]
    h = _norm_mod(x, nm_ref[...], sc_ref[...], sh_ref[...])
    hb = h.reshape(tm, D).astype(BF16)

    xp = _dot(hb, win_ref[:, 0:D])
    _extend(xp, cp_ref, ep_ref)
    row = lax.broadcasted_iota(jnp.int32, (tm, LANES), 0)
    pos = j * tm + (row & ~(CHUNK - 1)) + _stored_time((tm, LANES), 0)
    d_parts = []
    for g, w in enumerate(POOL_WINDOWS):
        s = _window_sum(ep_ref, ta_ref, tb_ref, g * POOL_GW, w)
        inv = 1.0 / jnp.minimum(w, pos + 1).astype(F32)
        inv = jnp.concatenate([inv] * (POOL_GW // LANES), axis=1)
        d_parts.append(s * inv - xp[:, g * POOL_GW:(g + 1) * POOL_GW])
    y_pool = _pool_project(d_parts, poolw_ref, pscale_ref[...])

    xl = _dot(hb, win_ref[:, D:2 * D])
    _extend(xl, cc_ref, ec_ref)
    cw = convw_ref[...]
    xc = convb_ref[...] + cw[CONV_W - 1:CONV_W, :] * xl
    for k in range(CONV_W - 1):
        xc = xc + cw[k:k + 1, :] * _shifted(ec_ref, CONV_W - 1 - k)

    @pl.when(j == nt - 1)
    def _():
        tailp_ref[...] = xp[tm - CHUNK:, :]
        tailc_ref[...] = xl[tm - CHUNK:, :]

    a, bx = _lru_coeffs(xc, wri_ref, br_ref[...], bi_ref[...], lam_ref[...])
    a3 = a.reshape(nc, CHUNK, D)
    b3 = bx.reshape(nc, CHUNK, D)
    hq = b3[:, 0:SUBLANES, :]
    paq = a3[:, 0:SUBLANES, :]
    pa_ref[:, 0:SUBLANES, :] = paq
    h_ref[:, 0:SUBLANES, :] = hq
    for q in range(1, SEG):
        rows = slice(q * SUBLANES, (q + 1) * SUBLANES)
        hq = a3[:, rows, :] * hq + b3[:, rows, :]
        paq = a3[:, rows, :] * paq
        pa_ref[:, rows, :] = paq
        h_ref[:, rows, :] = hq
    g_state = hc_ref[...]
    entry = []
    for c in range(nc):
        rows_c = []
        for s in range(SUBLANES):
            rows_c.append(g_state)
            g_state = paq[c, s:s + 1, :] * g_state + hq[c, s:s + 1, :]
        entry.append(jnp.concatenate(rows_c, axis=0))
    entry = jnp.stack(entry, axis=0)
    hc_ref[...] = g_state

    @pl.when(j == nt - 1)
    def _():
        nh_ref[...] = g_state

    for q in range(SEG):
        rows = slice(q * SUBLANES, (q + 1) * SUBLANES)
        hs_ref[:, rows, :] = pa_ref[:, rows, :] * entry + h_ref[:, rows, :]

    gl = _dot(hb, win_ref[:, 2 * D:3 * D])
    y_lru = hs_ref[...].reshape(tm, D) * jax.nn.gelu(gl)

    u = _dot(hb, win_ref[:, 3 * D:3 * D + GMLP_W])
    v = _dot(hb, win_ref[:, 3 * D + GMLP_W:4 * D])
    vnb = _rms(v, gnorm_ref[...]).astype(BF16)
    tri = _stored_time((CHUNK, CHUNK), 0) >= _stored_time((CHUNK, CHUNK), 1)
    gbs = gbs_ref[...]
    for g in range(GMLP_W // GMLP_GW):
        wm = jnp.where(tri, gws_ref[g], 0.0).astype(BF16)
        cols = slice(g * GMLP_GW, (g + 1) * GMLP_GW)
        for c in range(nc):
            rows = slice(c * CHUNK, (c + 1) * CHUNK)
            s = _dot(wm, vnb[rows, cols]) + gbs[:, cols]
            yg_ref[rows, cols] = u[rows, cols] * s
    y_gm = yg_ref[...]

    y = _merge_out(hb, win_ref, y_pool, y_lru, y_gm, wbop_ref, wbol_ref, wbog_ref, wout_ref)
    o_ref[...] = x + gt_ref[...] * y.reshape(1, tm, D)


def _mix_weight_specs(l):
    return [
        _row_spec(l, D),
        _full_spec(l, (D, IN_W)),
        _full_spec(l, (len(POOL_WINDOWS), POOL_GW, POOL_GW)),
        _row_spec(l, D),
        pl.BlockSpec((None, CONV_W, D), lambda i, j: (l, 0, 0)),
        _row_spec(l, D),
        _full_spec(l, (LRU_HEADS, LRU_HD, 2 * LRU_HD)),
        _row_spec(l, D), _row_spec(l, D), _row_spec(l, D),
        _row_spec(l, GMLP_W),
    ]


def _mix_proj_specs(l):
    return [_full_spec(l, (D, D)), _full_spec(l, (D, D)), _full_spec(l, (GMLP_W, D)),
            _full_spec(l, (D, D))]


def _mix_prompt(x, mods, l, wts, gws, gbs_rows, *, tm):
    nbt, tt, _ = x.shape
    grid = (nbt, tt // tm)
    xspec = pl.BlockSpec((1, tm, D), lambda i, j: (i, j, 0))
    in_specs = ([xspec, _mod_spec(l, 3, 1), _mod_spec(l, 4, 1), _mod_spec(l, 5, 1)]
                + _mix_weight_specs(l)
                + [pl.BlockSpec((None, GMLP_W // GMLP_GW, CHUNK, CHUNK), lambda i, j: (l, 0, 0, 0)),
                   pl.BlockSpec((None, CHUNK, GMLP_W), lambda i, j: (l, 0, 0))]
                + _mix_proj_specs(l))
    nc = tm // CHUNK
    out_specs = [
        xspec,
        pl.BlockSpec((None, CHUNK, D), lambda i, j: (i, 0, 0)),
        pl.BlockSpec((None, CHUNK, D), lambda i, j: (i, 0, 0)),
        pl.BlockSpec((None, 1, D), lambda i, j: (i, 0, 0)),
    ]
    out_shape = [
        jax.ShapeDtypeStruct(x.shape, F32),
        jax.ShapeDtypeStruct((nbt, CHUNK, D), F32),
        jax.ShapeDtypeStruct((nbt, CHUNK, D), F32),
        jax.ShapeDtypeStruct((nbt, 1, D), F32),
    ]
    scratch = [
        pltpu.VMEM((nc, 2 * CHUNK, D), F32),
        pltpu.VMEM((CHUNK, D), F32),
        pltpu.VMEM((nc, 2 * CHUNK, POOL_GW), F32),
        pltpu.VMEM((nc, 2 * CHUNK, POOL_GW), F32),
        pltpu.VMEM((nc, 2 * CHUNK, D), F32),
        pltpu.VMEM((CHUNK, D), F32),
        pltpu.VMEM((nc, CHUNK, D), F32),
        pltpu.VMEM((nc, CHUNK, D), F32),
        pltpu.VMEM((nc, CHUNK, D), F32),
        pltpu.VMEM((1, D), F32),
        pltpu.VMEM((tm, GMLP_W), F32),
    ]
    (norm, w_in, pool_w, pool_scale, conv_w, conv_b, wri, br, bi, lam, gnorm,
     wbop, wbol, wbog, wout) = wts
    return pl.pallas_call(
        functools.partial(_mix_prompt_body, tm=tm),
        grid=grid,
        in_specs=in_specs,
        out_specs=out_specs,
        out_shape=out_shape,
        scratch_shapes=scratch,
        compiler_params=pltpu.CompilerParams(
            dimension_semantics=("arbitrary", "arbitrary"), vmem_limit_bytes=VMEM_LIMIT),
        name="mix_prompt",
    )(x, mods, mods, mods, norm, w_in, pool_w, pool_scale, conv_w, conv_b, wri, br, bi, lam, gnorm,
      gws, gbs_rows, wbop, wbol, wbog, wout)


def _mix_sample_body(x_ref, sh_ref, sc_ref, gt_ref, nm_ref, win_ref, poolw_ref, pscale_ref,
                     convw_ref, convb_ref, wri_ref, br_ref, bi_ref, lam_ref, gnorm_ref, gws_ref,
                     gbs_ref, wbop_ref, wbol_ref, wbog_ref, wout_ref,
                     spool_ref, sconv_ref, sh0_ref,
                     o_ref, npool_ref, nconv_ref, nh_ref, vn_ref,
                     p_ref, c_ref, *, nb, t):
    rows = nb * t
    x = x_ref[...]
    h = _norm_mod(x, nm_ref[...], sc_ref[...], sh_ref[...])
    hb = h.reshape(rows, D).astype(BF16)

    xp = _dot(hb, win_ref[:, 0:D]).reshape(nb, t, D)
    p_ref[:, pl.ds(1, POOL_PREFIX), :] = spool_ref[...]
    p_ref[:, pl.ds(POOL_PREFIX + 1, t), :] = xp
    d_parts = []
    for g, w in enumerate(POOL_WINDOWS):
        cols = slice(g * POOL_GW, (g + 1) * POOL_GW)
        s = xp[:, :, cols]
        for k in range(1, w):
            s = s + p_ref[:, pl.ds(POOL_PREFIX + 1 - k, t), cols]
        d_parts.append((s / float(w) - xp[:, :, cols]).reshape(rows, POOL_GW))
    y_pool = _pool_project(d_parts, poolw_ref, pscale_ref[...])
    npool_ref[...] = p_ref[:, pl.ds(t + 1, POOL_PREFIX), :]

    xl = _dot(hb, win_ref[:, D:2 * D]).reshape(nb, t, D)
    c_ref[:, pl.ds(CPREF - (CONV_W - 1), CONV_W - 1), :] = sconv_ref[...]
    c_ref[:, pl.ds(CPREF, t), :] = xl
    cw = convw_ref[...]
    xc = convb_ref[...] + cw[3:4, :] * xl
    for k in range(CONV_W - 1):
        xc = xc + cw[k:k + 1, :] * c_ref[:, pl.ds(CPREF - (CONV_W - 1) + k, t), :]
    nconv_ref[...] = c_ref[:, pl.ds(CPREF + t - (CONV_W - 1), CONV_W - 1), :]

    a, bx = _lru_coeffs(xc.reshape(rows, D), wri_ref, br_ref[...], bi_ref[...], lam_ref[...])
    a, bx = _scan8(a, bx)
    hs = a.reshape(nb, t, D) * sh0_ref[...] + bx.reshape(nb, t, D)
    nh_ref[...] = hs[:, t - 1:t, :]
    gl = _dot(hb, win_ref[:, 2 * D:3 * D])
    y_lru = hs.reshape(rows, D) * jax.nn.gelu(gl)

    u = _dot(hb, win_ref[:, 3 * D:3 * D + GMLP_W])
    v = _dot(hb, win_ref[:, 3 * D + GMLP_W:4 * D])
    vn = _rms(v, gnorm_ref[...]).reshape(nb, t, GMLP_W)
    vn_ref[...] = vn
    tri = (lax.broadcasted_iota(jnp.int32, (t, t), 0) >= lax.broadcasted_iota(jnp.int32, (t, t), 1))
    gbs = gbs_ref[...]
    s_parts = []
    for g in range(GMLP_W // GMLP_GW):
        cols = slice(g * GMLP_GW, (g + 1) * GMLP_GW)
        wm = jnp.where(tri, gws_ref[g][0:t, 0:t], 0.0)
        s = jnp.broadcast_to(gbs[0:t, cols], (nb, t, GMLP_GW))
        for k in range(t):
            s = s + wm[:, k:k + 1] * vn[:, k:k + 1, cols]
        s_parts.append(s)
    y_gm = u * jnp.concatenate(s_parts, axis=2).reshape(rows, GMLP_W)

    y = _merge_out(hb, win_ref, y_pool, y_lru, y_gm, wbop_ref, wbol_ref, wbog_ref, wout_ref)
    o_ref[...] = x + gt_ref[...] * y.reshape(nb, t, D)


def _mix_sample(x, mods, l, wts, gws, gbs_rows, spool, sconv, sh0, *, nb):
    nbt, t, _ = x.shape
    grid = (nbt // nb, 1)
    xspec = pl.BlockSpec((nb, t, D), lambda i, j: (i, 0, 0))

    def state_spec(r):
        return pl.BlockSpec((None, nb, r, D), lambda i, j: (l, i, 0, 0))

    def out_state_spec(r):
        return pl.BlockSpec((nb, r, D), lambda i, j: (i, 0, 0))

    in_specs = ([xspec, _mod_spec(l, 3, nb), _mod_spec(l, 4, nb), _mod_spec(l, 5, nb)]
                + _mix_weight_specs(l)
                + [pl.BlockSpec((None, GMLP_W // GMLP_GW, CHUNK, CHUNK), lambda i, j: (l, 0, 0, 0)),
                   pl.BlockSpec((None, CHUNK, GMLP_W), lambda i, j: (l, 0, 0))]
                + _mix_proj_specs(l)
                + [state_spec(POOL_PREFIX), state_spec(CONV_W - 1), state_spec(1)])
    out_specs = [xspec, out_state_spec(POOL_PREFIX), out_state_spec(CONV_W - 1), out_state_spec(1),
                 pl.BlockSpec((nb, t, GMLP_W), lambda i, j: (i, 0, 0))]
    out_shape = [
        jax.ShapeDtypeStruct(x.shape, F32),
        jax.ShapeDtypeStruct((nbt, POOL_PREFIX, D), F32),
        jax.ShapeDtypeStruct((nbt, CONV_W - 1, D), F32),
        jax.ShapeDtypeStruct((nbt, 1, D), F32),
        jax.ShapeDtypeStruct((nbt, t, GMLP_W), F32),
    ]
    scratch = [
        pltpu.VMEM((nb, POOL_PREFIX + 1 + t, D), F32),
        pltpu.VMEM((nb, CPREF + t, D), F32),
    ]
    (norm, w_in, pool_w, pool_scale, conv_w, conv_b, wri, br, bi, lam, gnorm,
     wbop, wbol, wbog, wout) = wts
    return pl.pallas_call(
        functools.partial(_mix_sample_body, nb=nb, t=t),
        grid=grid,
        in_specs=in_specs,
        out_specs=out_specs,
        out_shape=out_shape,
        scratch_shapes=scratch,
        compiler_params=pltpu.CompilerParams(
            dimension_semantics=("arbitrary", "arbitrary"), vmem_limit_bytes=VMEM_LIMIT),
        name="mix_sample",
    )(x, mods, mods, mods, norm, w_in, pool_w, pool_scale, conv_w, conv_b, wri, br, bi, lam, gnorm,
      gws, gbs_rows, wbop, wbol, wbog, wout, spool, sconv, sh0)


def kernel(x_prompt, x_sample, c_prompt, c_sample, state_pool, state_conv, state_lru, w_ada, b_ada,
           norm_ffn1, ffn1_w_gate, ffn1_w_up, ffn1_w_down, norm_mix, w_in, pool_w, pool_scale,
           conv_w, conv_b, lru_wr, lru_br, lru_wi, lru_bi, lru_lambda, gmlp_norm, gmlp_ws, gmlp_bs,
           wbo_pool, wbo_lru, wbo_gmlp, w_out, norm_ffn2, ffn2_w_gate, ffn2_w_up, ffn2_w_down,
           norm_final):
    nbp, tp, _ = x_prompt.shape
    nbs, ts, _ = x_sample.shape
    tm_ffn, nb_ffn, tm_mix, nb_mix = 512, 64, 512, 32

    mods_p, mods_s = _ada(c_prompt, c_sample, w_ada, b_ada)

    def row(v):
        return v.reshape(DEPTH, 1, v.shape[-1])

    bf = lambda w: w.astype(BF16)
    f1 = (row(norm_ffn1), ffn1_w_gate, ffn1_w_up, ffn1_w_down)
    f2 = (row(norm_ffn2), ffn2_w_gate, ffn2_w_up, ffn2_w_down)
    wri = bf(0.5 * jnp.concatenate([lru_wr, lru_wi], axis=-1))
    col_scale = jnp.where(jnp.arange(IN_W) >= GATE_COL0, 0.5, 1.0).astype(F32)
    mixw = (row(norm_mix), bf(w_in * col_scale), bf(pool_w), row(pool_scale), conv_w, row(conv_b),
            wri, row(lru_br), row(lru_bi), row(lru_lambda), row(gmlp_norm),
            bf(0.5 * wbo_pool), bf(0.5 * wbo_lru), bf(0.5 * wbo_gmlp), bf(w_out))
    gbs_rows = jnp.repeat(jnp.swapaxes(gmlp_bs, 1, 2), GMLP_GW, axis=2)
    tperm = np.array([(r % SUBLANES) * SEG + r // SUBLANES for r in range(CHUNK)])
    gws_p = gmlp_ws[:, :, tperm][:, :, :, tperm]
    gbs_rows_p = gbs_rows[:, tperm, :]
    nf = norm_final.reshape(1, D)
    sh0 = state_lru.reshape(DEPTH, nbs, 1, D)

    xp = x_prompt.reshape(nbp, tp // CHUNK, SUBLANES, SEG, D).swapaxes(2, 3).reshape(nbp, tp, D)
    xs = x_sample
    pp, pc, ph, sp, sc, sh, sv = [], [], [], [], [], [], []
    for l in range(DEPTH):
        final = l == DEPTH - 1
        xp, xs = _ffn(xp, xs, mods_p, mods_s, l, 0, *f1, nf, tm=tm_ffn, nb=nb_ffn, final=False)
        xp, tailp, tailc, nh = _mix_prompt(xp, mods_p, l, mixw, gws_p, gbs_rows_p, tm=tm_mix)
        npool = tailp[:, SUBLANES - 1::SUBLANES][:, SEG - POOL_PREFIX:]
        nconv = tailc[:, SUBLANES - 1::SUBLANES][:, SEG - (CONV_W - 1):]
        pp.append(npool); pc.append(nconv); ph.append(nh.reshape(nbp, D))
        xs, npool, nconv, nh, nv = _mix_sample(xs, mods_s, l, mixw, gmlp_ws, gbs_rows,
                                               state_pool, state_conv, sh0, nb=nb_mix)
        xp, xs = _ffn(xp, xs, mods_p, mods_s, l, 6, *f2, nf, tm=tm_ffn, nb=nb_ffn, final=final)
        sp.append(npool); sc.append(nconv); sh.append(nh.reshape(nbs, D)); sv.append(nv)
    xp = xp.reshape(nbp, tp // CHUNK, SEG, SUBLANES, D).swapaxes(2, 3).reshape(nbp, tp, D)
    return (xp, xs, jnp.stack(pp), jnp.stack(pc), jnp.stack(ph),
            jnp.stack(sp), jnp.stack(sc), jnp.stack(sh), jnp.stack(sv))
```

```python
import functools

import numpy as np
import jax
import jax.numpy as jnp
from jax import lax
from jax.experimental import pallas as pl
from jax.experimental.pallas import tpu as pltpu

D = 1024
DEPTH = 4
PAST_LEN = 16384
POOL_WINDOWS = (2, 4, 8, 16)
POOL_GW = 256
POOL_PREFIX = 15
LRU_HEADS = 8
LRU_HD = 128
CONV_W = 4
LRU_C = 8.0
GMLP_W = 512
GMLP_GW = 128
CHUNK = 128
IN_W = 7168
D_FF = 2816
N_MOD = 9
EPS = 1e-6

GATE_COL0 = 3 * D + 2 * GMLP_W
LOG2E = 1.4426950408889634

FF_CHUNK = 256
N_FF_CHUNKS = D_FF // FF_CHUNK
NEXT_INPUT_PIECES = 8
SUBLANES = 8
LANES = 128
SEG = CHUNK // SUBLANES
CPREF = 8
VMEM_LIMIT = 56 * 1024 * 1024

F32 = jnp.float32
BF16 = jnp.bfloat16


def _dot(a, b):
    return jnp.dot(a, b, preferred_element_type=F32)


def _norm_mod(x, n, sc, sh):
    ms = jnp.mean(x * x, axis=-1, keepdims=True)
    y = x * lax.rsqrt(ms + EPS) * n
    return y * (1.0 + sc) + sh


def _ada_body(cp_ref, cs_ref, w_ref, b_ref, op_ref, os_ref):
    w = w_ref[...].astype(BF16)
    b = b_ref[...]

    def f(c):
        s = c * jax.nn.sigmoid(c)
        return _dot(s.astype(BF16), w) + b

    op_ref[...] = f(cp_ref[...])[:, None, :]
    os_ref[...] = f(cs_ref[...])[:, None, :]


def _ada(c_prompt, c_sample, w_ada, b_ada):
    nbp, nbs = c_prompt.shape[0], c_sample.shape[0]
    return pl.pallas_call(
        _ada_body,
        grid=(DEPTH, N_MOD),
        in_specs=[
            pl.BlockSpec((nbp, D), lambda l, j: (0, 0)),
            pl.BlockSpec((nbs, D), lambda l, j: (0, 0)),
            pl.BlockSpec((None, D, D), lambda l, j: (l, 0, j)),
            pl.BlockSpec((None, 1, D), lambda l, j: (l, 0, j)),
        ],
        out_specs=[
            pl.BlockSpec((None, None, nbp, 1, D), lambda l, j: (l, j, 0, 0, 0)),
            pl.BlockSpec((None, None, nbs, 1, D), lambda l, j: (l, j, 0, 0, 0)),
        ],
        out_shape=[
            jax.ShapeDtypeStruct((DEPTH, N_MOD, nbp, 1, D), F32),
            jax.ShapeDtypeStruct((DEPTH, N_MOD, nbs, 1, D), F32),
        ],
        name="ada",
    )(c_prompt, c_sample, w_ada, b_ada.reshape(DEPTH, 1, N_MOD * D))


def _ffn_input(x_ref, sh_ref, sc_ref, n_ref, rows=slice(None)):
    x = x_ref[:, rows, :]
    h = _norm_mod(x, n_ref[...], sc_ref[...], sh_ref[...])
    return h.reshape(x.shape[0] * x.shape[1], D).astype(BF16)


def _ffn_tile(x_ref, hb_get, gt_ref, wg_s, wu_s, wd_s, nf_ref, o_ref, final, chunk_hook=None):
    x = x_ref[...]
    nb, t, _ = x.shape
    rows = nb * t
    acc = jnp.zeros((rows, D), F32)
    for c in range(N_FF_CHUNKS):
        if chunk_hook is not None:
            chunk_hook(c)
        hb = hb_get()
        g = _dot(hb, wg_s[c])
        u = _dot(hb, wu_s[c])
        a = (g * jax.nn.sigmoid(g)) * u
        acc = acc + _dot(a.astype(BF16), wd_s[c])
    out = x + (0.5 * gt_ref[...]) * acc.reshape(nb, t, D)
    if final:
        ms = jnp.mean(out * out, axis=-1, keepdims=True)
        out = out * lax.rsqrt(ms + EPS) * nf_ref[...]
    o_ref[...] = out


def _ffn_body(xp_ref, shp_ref, scp_ref, gtp_ref, xn_ref, shn_ref, scn_ref,
              xs_ref, shs_ref, scs_ref, gts_ref, n_ref,
              wg_ref, wu_ref, wd_ref, nf_ref, op_ref, os_ref, wg_s, wu_s, wd_s, hb_s, *, n_p, final):
    i = pl.program_id(0)

    @pl.when(i < N_FF_CHUNKS)
    def _():
        wg_s[i] = wg_ref[...].astype(BF16)
        wu_s[i] = wu_ref[...].astype(BF16)
        wd_s[i] = wd_ref[...].astype(BF16)

    @pl.when(i == N_FF_CHUNKS - 1)
    def _():
        hb_s[N_FF_CHUNKS % 2] = _ffn_input(xp_ref, shp_ref, scp_ref, n_ref)

    @pl.when(jnp.logical_and(i >= N_FF_CHUNKS, i < N_FF_CHUNKS + n_p))
    def _():
        slot = lax.rem(i, 2)

        tm = xp_ref.shape[1]
        piece = tm // NEXT_INPUT_PIECES

        def next_input(c):
            if c < NEXT_INPUT_PIECES:
                rows = slice(c * piece, (c + 1) * piece)
                hb_s[1 - slot, rows, :] = _ffn_input(xn_ref, shn_ref, scn_ref, n_ref, rows)

        _ffn_tile(xp_ref, lambda: hb_s[slot], gtp_ref, wg_s, wu_s, wd_s, nf_ref, op_ref, final,
                  chunk_hook=next_input)

    @pl.when(i >= N_FF_CHUNKS + n_p)
    def _():
        hb = _ffn_input(xs_ref, shs_ref, scs_ref, n_ref)
        _ffn_tile(xs_ref, lambda: hb, gts_ref, wg_s, wu_s, wd_s, nf_ref, os_ref, final)


def _ffn(xp, xs, mods_p, mods_s, l, k0, norm, wg, wu, wd, norm_final, *, tm, nb, final):
    nbp, tp, _ = xp.shape
    nbs, ts, _ = xs.shape
    tpb = tp // tm
    n_p, n_s = nbp * tpb, nbs // nb

    def p_tile(i):
        return jnp.clip(i - N_FF_CHUNKS, 0, n_p - 1)

    def s_tile(i):
        return jnp.clip(i - N_FF_CHUNKS - n_p, 0, n_s - 1)

    def w_chunk(i):
        return jnp.minimum(i, N_FF_CHUNKS - 1)

    def modp(k, ahead=0):
        return pl.BlockSpec((None, None, 1, 1, D),
                            lambda i: (l, k, p_tile(i + ahead) // tpb, 0, 0))

    def mods(k):
        return pl.BlockSpec((None, None, nb, 1, D), lambda i: (l, k, s_tile(i), 0, 0))

    xp_spec = pl.BlockSpec((1, tm, D), lambda i: (p_tile(i) // tpb, p_tile(i) % tpb, 0))
    xn_spec = pl.BlockSpec((1, tm, D), lambda i: (p_tile(i + 1) // tpb, p_tile(i + 1) % tpb, 0))
    xs_spec = pl.BlockSpec((nb, ts, D), lambda i: (s_tile(i), 0, 0))
    w_col = pl.BlockSpec((None, D, FF_CHUNK), lambda i: (l, 0, w_chunk(i)))
    w_row = pl.BlockSpec((None, FF_CHUNK, D), lambda i: (l, w_chunk(i), 0))
    return pl.pallas_call(
        functools.partial(_ffn_body, n_p=n_p, final=final),
        grid=(N_FF_CHUNKS + n_p + n_s,),
        in_specs=[
            xp_spec, modp(k0), modp(k0 + 1), modp(k0 + 2),
            xn_spec, modp(k0, 1), modp(k0 + 1, 1),
            xs_spec, mods(k0), mods(k0 + 1), mods(k0 + 2),
            pl.BlockSpec((None, 1, D), lambda i: (l, 0, 0)),
            w_col, w_col, w_row,
            pl.BlockSpec((1, D), lambda i: (0, 0)),
        ],
        out_specs=[xp_spec, xs_spec],
        out_shape=[jax.ShapeDtypeStruct(xp.shape, F32), jax.ShapeDtypeStruct(xs.shape, F32)],
        scratch_shapes=[
            pltpu.VMEM((N_FF_CHUNKS, D, FF_CHUNK), BF16),
            pltpu.VMEM((N_FF_CHUNKS, D, FF_CHUNK), BF16),
            pltpu.VMEM((N_FF_CHUNKS, FF_CHUNK, D), BF16),
            pltpu.VMEM((2, tm, D), BF16),
        ],
        compiler_params=pltpu.CompilerParams(
            dimension_semantics=("arbitrary",), vmem_limit_bytes=VMEM_LIMIT),
        name="ffn_final" if final else "ffn",
    )(xp, mods_p, mods_p, mods_p, xp, mods_p, mods_p, xs, mods_s, mods_s, mods_s,
      norm, wg, wu, wd, norm_final)


def _mod_spec(l, k, nb):
    return pl.BlockSpec((None, None, nb, 1, D), lambda i, j: (l, k, i, 0, 0))


def _row_spec(l, width):
    return pl.BlockSpec((None, 1, width), lambda i, j: (l, 0, 0))


def _full_spec(l, shape):
    nd = len(shape)
    return pl.BlockSpec((None,) + tuple(shape), lambda i, j: (l,) + (0,) * nd,
                        pipeline_mode=pl.Buffered(1))


def _scan8(a, b):
    row = lax.broadcasted_iota(jnp.int32, a.shape, 0) & (SUBLANES - 1)
    for d in (1, 2, 4):
        m = row >= d
        a_s = jnp.where(m, pltpu.roll(a, d, 0), 1.0)
        b_s = jnp.where(m, pltpu.roll(b, d, 0), 0.0)
        b = a * b_s + b
        a = a * a_s
    return a, b


def _lru_coeffs(xc, wri_ref, br, bi, lam, head0=0):
    xcb = xc.astype(BF16)
    rs, is_ = [], []
    for hd in range(xc.shape[1] // LRU_HD):
        res = _dot(xcb[:, hd * LRU_HD:(hd + 1) * LRU_HD], wri_ref[head0 + hd])
        rs.append(res[:, :LRU_HD])
        is_.append(res[:, LRU_HD:])
    tr = jnp.tanh(jnp.concatenate(rs, axis=1) + 0.5 * br)
    ti = jnp.tanh(jnp.concatenate(is_, axis=1) + 0.5 * bi)
    nl = -lam
    softplus = jnp.maximum(nl, 0.0) + jnp.log1p(jnp.exp(-jnp.abs(nl)))
    c1 = (-0.5 * LRU_C * LOG2E) * softplus
    a = jnp.exp2(c1 * (tr + 1.0))
    y = 1.0 - a * a
    root = jnp.where(y > 0.0, y * lax.rsqrt(y), 0.0)
    bx = root * ((0.5 * ti + 0.5) * xc)
    return a, bx


def _pool_project(d_parts, poolw_ref, pscale):
    ys = [_dot(d.astype(BF16), poolw_ref[g]) for g, d in enumerate(d_parts)]
    return jnp.concatenate(ys, axis=1) * pscale


def _rms(v, g):
    ms = jnp.mean(v * v, axis=-1, keepdims=True)
    return v * lax.rsqrt(ms + EPS) * g


def _merge_out(hb_get, win_ref, t_pool, t_lru, y_gm, wbog_ref, wout_ref):
    def gate(k):
        c0 = GATE_COL0 + k * D
        return jnp.tanh(_dot(hb_get(), win_ref[:, c0:c0 + D])) + 1.0

    merged = gate(0) * t_pool
    merged = merged + gate(1) * t_lru
    merged = merged + gate(2) * _dot(y_gm.astype(BF16), wbog_ref[...])
    return _dot(merged.astype(BF16), wout_ref[...])


def _extend(x, carry_ref, e_ref, cols):
    tm, cdim = x.shape
    nc = tm // CHUNK
    xr = pltpu.roll(x.reshape(tm // SUBLANES, SUBLANES, cdim), 1, 1).reshape(tm, cdim)
    prev = carry_ref[:, cols]
    if nc > 1:
        prev = jnp.concatenate([prev, xr[:tm - CHUNK]], axis=0)
    sub0 = (lax.broadcasted_iota(jnp.int32, (tm, cdim), 0) & (SUBLANES - 1)) == 0
    e_ref[:, 0:CHUNK, cols] = jnp.where(sub0, prev, xr).reshape(nc, CHUNK, cdim)
    e_ref[:, CHUNK:2 * CHUNK, cols] = x.reshape(nc, CHUNK, cdim)
    carry_ref[:, cols] = xr[tm - CHUNK:]


def _shifted(e_ref, k, cols=slice(None)):
    v = e_ref[:, pl.ds((SEG - k) * SUBLANES, CHUNK), cols]
    return v.reshape(v.shape[0] * CHUNK, v.shape[2])


def _window_sum(e_ref, ta_ref, tb_ref, c0, w):
    src, cols = e_ref, slice(c0, c0 + POOL_GW)
    bufs = (ta_ref, tb_ref)
    k, level, lo = 1, 0, 0
    while True:
        last = 2 * k == w
        lo = lo + k
        start = SEG if last else lo
        n = (2 * SEG - start) * SUBLANES
        val = (src[:, pl.ds(start * SUBLANES, n), cols]
               + src[:, pl.ds((start - k) * SUBLANES, n), cols])
        if last:
            return val.reshape(val.shape[0] * CHUNK, POOL_GW)
        dst = bufs[level % 2]
        dst[:, pl.ds(start * SUBLANES, n), :] = val
        src, cols = dst, slice(0, POOL_GW)
        k, level = 2 * k, level + 1


def _stored_time(shape, axis):
    r = lax.broadcasted_iota(jnp.int32, shape, axis)
    return (r & (SUBLANES - 1)) * SEG + ((r >> 3) & (SEG - 1))


def _mix_prompt_body(x_ref, sh_ref, sc_ref, gt_ref, xn_ref, shn_ref, scn_ref,
                     nm_ref, win_ref, poolw_ref, pscale_ref,
                     convw_ref, convb_ref, wri_ref, br_ref, bi_ref, lam_ref, gnorm_ref, gws_ref,
                     gbs_ref, wbop_ref, wbol_ref, wbog_ref, wout_ref,
                     o_ref, tailp_ref, tailc_ref, nh_ref,
                     ep_ref, cp_ref, ta_ref, tb_ref, ec_ref, cc_ref, pa_ref, h_ref, hc_ref,
                     yg_ref, hb_s, *, tm):
    j = pl.program_id(1)
    nt = pl.num_programs(1)
    nc = tm // CHUNK
    step = pl.program_id(0) * nt + j
    slot = lax.rem(step, 2)

    @pl.when(j == 0)
    def _():
        cp_ref[...] = jnp.zeros((CHUNK, D), F32)
        cc_ref[...] = jnp.zeros((CHUNK, D), F32)
        hc_ref[...] = jnp.zeros((1, D), F32)

    @pl.when(step == 0)
    def _():
        hb_s[0] = _ffn_input(x_ref, sh_ref, sc_ref, nm_ref)

    x = x_ref[...]
    piece = tm // NEXT_INPUT_PIECES

    def hb_get():
        return hb_s[slot]

    def next_input(k):
        rows = slice(k * piece, (k + 1) * piece)
        hb_s[1 - slot, rows, :] = _ffn_input(xn_ref, shn_ref, scn_ref, nm_ref, rows)

    all_cols = slice(0, D)
    xp = _dot(hb_get(), win_ref[:, 0:D])
    next_input(0)
    _extend(xp, cp_ref, ep_ref, all_cols)
    row = lax.broadcasted_iota(jnp.int32, (tm, LANES), 0)
    pos = j * tm + (row & ~(CHUNK - 1)) + _stored_time((tm, LANES), 0)
    d_parts = []
    for g, w in enumerate(POOL_WINDOWS):
        s = _window_sum(ep_ref, ta_ref, tb_ref, g * POOL_GW, w)
        inv = 1.0 / jnp.minimum(w, pos + 1).astype(F32)
        inv = jnp.concatenate([inv] * (POOL_GW // LANES), axis=1)
        d_parts.append(s * inv - xp[:, g * POOL_GW:(g + 1) * POOL_GW])
    y_pool = _pool_project(d_parts, poolw_ref, pscale_ref[...])
    next_input(1)

    xl = _dot(hb_get(), win_ref[:, D:2 * D])
    next_input(2)
    _extend(xl, cc_ref, ec_ref, all_cols)
    cw = convw_ref[...]
    xc = convb_ref[...] + cw[CONV_W - 1:CONV_W, :] * xl
    for k in range(CONV_W - 1):
        xc = xc + cw[k:k + 1, :] * _shifted(ec_ref, CONV_W - 1 - k)

    tailp_ref[...] = xp[tm - CHUNK:, :].reshape(SEG, SUBLANES, D)[:, SUBLANES - 1, :]
    tailc_ref[...] = xl[tm - CHUNK:, :].reshape(SEG, SUBLANES, D)[:, SUBLANES - 1, :]

    a, bx = _lru_coeffs(xc, wri_ref, br_ref[...], bi_ref[...], lam_ref[...])
    next_input(3)
    a3 = a.reshape(nc, CHUNK, D)
    b3 = bx.reshape(nc, CHUNK, D)
    hq = b3[:, 0:SUBLANES, :]
    paq = a3[:, 0:SUBLANES, :]
    pa_ref[:, 0:SUBLANES, :] = paq
    h_ref[:, 0:SUBLANES, :] = hq
    for q in range(1, SEG):
        rows = slice(q * SUBLANES, (q + 1) * SUBLANES)
        hq = a3[:, rows, :] * hq + b3[:, rows, :]
        paq = a3[:, rows, :] * paq
        pa_ref[:, rows, :] = paq
        h_ref[:, rows, :] = hq
    g_state = hc_ref[...]
    entry = []
    for c in range(nc):
        rows_c = []
        for s in range(SUBLANES):
            rows_c.append(g_state)
            g_state = paq[c, s:s + 1, :] * g_state + hq[c, s:s + 1, :]
        entry.append(jnp.concatenate(rows_c, axis=0))
    entry = jnp.stack(entry, axis=0)
    hc_ref[...] = g_state
    nh_ref[...] = g_state

    hs = [pa_ref[:, q * SUBLANES:(q + 1) * SUBLANES, :] * entry
          + h_ref[:, q * SUBLANES:(q + 1) * SUBLANES, :] for q in range(SEG)]
    hs = jnp.concatenate(hs, axis=1).reshape(tm, D)
    next_input(4)
    gl = _dot(hb_get(), win_ref[:, 2 * D:3 * D])
    y_lru = hs * jax.nn.gelu(gl)
    t_pool = _dot(y_pool.astype(BF16), wbop_ref[...])
    t_lru = _dot(y_lru.astype(BF16), wbol_ref[...])

    next_input(5)
    u = _dot(hb_get(), win_ref[:, 3 * D:3 * D + GMLP_W])
    next_input(6)
    v = _dot(hb_get(), win_ref[:, 3 * D + GMLP_W:4 * D])
    vnb = _rms(v, gnorm_ref[...]).astype(BF16)
    tri = _stored_time((CHUNK, CHUNK), 0) >= _stored_time((CHUNK, CHUNK), 1)
    gbs = gbs_ref[...]
    for g in range(GMLP_W // GMLP_GW):
        wm = jnp.where(tri, gws_ref[g], 0.0).astype(BF16)
        cols = slice(g * GMLP_GW, (g + 1) * GMLP_GW)
        for c in range(nc):
            rows = slice(c * CHUNK, (c + 1) * CHUNK)
            s = _dot(wm, vnb[rows, cols]) + gbs[:, cols]
            yg_ref[rows, cols] = u[rows, cols] * s
    y_gm = yg_ref[...]
    next_input(7)

    y = _merge_out(hb_get, win_ref, t_pool, t_lru, y_gm, wbog_ref, wout_ref)
    o_ref[...] = x + gt_ref[...] * y.reshape(1, tm, D)


def _mix_weight_specs(l):
    return [
        _row_spec(l, D),
        _full_spec(l, (D, IN_W)),
        _full_spec(l, (len(POOL_WINDOWS), POOL_GW, POOL_GW)),
        _row_spec(l, D),
        pl.BlockSpec((None, CONV_W, D), lambda i, j: (l, 0, 0)),
        _row_spec(l, D),
        _full_spec(l, (LRU_HEADS, LRU_HD, 2 * LRU_HD)),
        _row_spec(l, D), _row_spec(l, D), _row_spec(l, D),
        _row_spec(l, GMLP_W),
    ]


def _mix_proj_specs(l):
    return [_full_spec(l, (D, D)), _full_spec(l, (D, D)), _full_spec(l, (GMLP_W, D)),
            _full_spec(l, (D, D))]


def _mix_prompt(x, mods, l, wts, gws, gbs_rows, *, tm):
    nbt, tt, _ = x.shape
    grid = (nbt, tt // tm)
    nt = tt // tm
    last = nbt * nt - 1
    xspec = pl.BlockSpec((1, tm, D), lambda i, j: (i, j, 0))

    def nxt(i, j):
        s2 = jnp.minimum(i * nt + j + 1, last)
        return s2 // nt, s2 % nt

    xn_spec = pl.BlockSpec((1, tm, D), lambda i, j: (*nxt(i, j), 0))

    def modn(k):
        return pl.BlockSpec((None, None, 1, 1, D), lambda i, j: (l, k, nxt(i, j)[0], 0, 0))

    in_specs = ([xspec, _mod_spec(l, 3, 1), _mod_spec(l, 4, 1), _mod_spec(l, 5, 1),
                 xn_spec, modn(3), modn(4)]
                + _mix_weight_specs(l)
                + [pl.BlockSpec((None, GMLP_W // GMLP_GW, CHUNK, CHUNK), lambda i, j: (l, 0, 0, 0)),
                   pl.BlockSpec((None, CHUNK, GMLP_W), lambda i, j: (l, 0, 0))]
                + _mix_proj_specs(l))
    nc = tm // CHUNK
    out_specs = [
        xspec,
        pl.BlockSpec((None, SEG, D), lambda i, j: (i, 0, 0)),
        pl.BlockSpec((None, SEG, D), lambda i, j: (i, 0, 0)),
        pl.BlockSpec((None, 1, D), lambda i, j: (i, 0, 0)),
    ]
    out_shape = [
        jax.ShapeDtypeStruct(x.shape, F32),
        jax.ShapeDtypeStruct((nbt, SEG, D), F32),
        jax.ShapeDtypeStruct((nbt, SEG, D), F32),
        jax.ShapeDtypeStruct((nbt, 1, D), F32),
    ]
    scratch = [
        pltpu.VMEM((nc, 2 * CHUNK, D), F32),
        pltpu.VMEM((CHUNK, D), F32),
        pltpu.VMEM((nc, 2 * CHUNK, POOL_GW), F32),
        pltpu.VMEM((nc, 2 * CHUNK, POOL_GW), F32),
        pltpu.VMEM((nc, 2 * CHUNK, D), F32),
        pltpu.VMEM((CHUNK, D), F32),
        pltpu.VMEM((nc, CHUNK, D), F32),
        pltpu.VMEM((nc, CHUNK, D), F32),
        pltpu.VMEM((1, D), F32),
        pltpu.VMEM((tm, GMLP_W), F32),
        pltpu.VMEM((2, tm, D), BF16),
    ]
    (norm, w_in, pool_w, pool_scale, conv_w, conv_b, wri, br, bi, lam, gnorm,
     wbop, wbol, wbog, wout) = wts
    return pl.pallas_call(
        functools.partial(_mix_prompt_body, tm=tm),
        grid=grid,
        in_specs=in_specs,
        out_specs=out_specs,
        out_shape=out_shape,
        scratch_shapes=scratch,
        compiler_params=pltpu.CompilerParams(
            dimension_semantics=("arbitrary", "arbitrary"), vmem_limit_bytes=VMEM_LIMIT),
        name="mix_prompt",
    )(x, mods, mods, mods, x, mods, mods, norm, w_in, pool_w, pool_scale, conv_w, conv_b, wri, br, bi, lam, gnorm,
      gws, gbs_rows, wbop, wbol, wbog, wout)


def _mix_sample_body(x_ref, sh_ref, sc_ref, gt_ref, nm_ref, win_ref, poolw_ref, pscale_ref,
                     convw_ref, convb_ref, wri_ref, br_ref, bi_ref, lam_ref, gnorm_ref, gws_ref,
                     gbs_ref, wbop_ref, wbol_ref, wbog_ref, wout_ref,
                     spool_ref, sconv_ref, sh0_ref,
                     o_ref, npool_ref, nconv_ref, nh_ref, vn_ref,
                     p_ref, c_ref, *, nb, t):
    rows = nb * t
    x = x_ref[...]
    h = _norm_mod(x, nm_ref[...], sc_ref[...], sh_ref[...])
    hb = h.reshape(rows, D).astype(BF16)

    xp = _dot(hb, win_ref[:, 0:D]).reshape(nb, t, D)
    p_ref[:, pl.ds(1, POOL_PREFIX), :] = spool_ref[...]
    p_ref[:, pl.ds(POOL_PREFIX + 1, t), :] = xp
    d_parts = []
    for g, w in enumerate(POOL_WINDOWS):
        cols = slice(g * POOL_GW, (g + 1) * POOL_GW)
        s = xp[:, :, cols]
        for k in range(1, w):
            s = s + p_ref[:, pl.ds(POOL_PREFIX + 1 - k, t), cols]
        d_parts.append((s / float(w) - xp[:, :, cols]).reshape(rows, POOL_GW))
    y_pool = _pool_project(d_parts, poolw_ref, pscale_ref[...])
    npool_ref[...] = p_ref[:, pl.ds(t + 1, POOL_PREFIX), :]

    xl = _dot(hb, win_ref[:, D:2 * D]).reshape(nb, t, D)
    c_ref[:, pl.ds(CPREF - (CONV_W - 1), CONV_W - 1), :] = sconv_ref[...]
    c_ref[:, pl.ds(CPREF, t), :] = xl
    cw = convw_ref[...]
    xc = convb_ref[...] + cw[3:4, :] * xl
    for k in range(CONV_W - 1):
        xc = xc + cw[k:k + 1, :] * c_ref[:, pl.ds(CPREF - (CONV_W - 1) + k, t), :]
    nconv_ref[...] = c_ref[:, pl.ds(CPREF + t - (CONV_W - 1), CONV_W - 1), :]

    a, bx = _lru_coeffs(xc.reshape(rows, D), wri_ref, br_ref[...], bi_ref[...], lam_ref[...])
    a, bx = _scan8(a, bx)
    hs = a.reshape(nb, t, D) * sh0_ref[...] + bx.reshape(nb, t, D)
    nh_ref[...] = hs[:, t - 1:t, :]
    gl = _dot(hb, win_ref[:, 2 * D:3 * D])
    y_lru = hs.reshape(rows, D) * jax.nn.gelu(gl)

    u = _dot(hb, win_ref[:, 3 * D:3 * D + GMLP_W])
    v = _dot(hb, win_ref[:, 3 * D + GMLP_W:4 * D])
    vn = _rms(v, gnorm_ref[...]).reshape(nb, t, GMLP_W)
    vn_ref[...] = vn
    tri = (lax.broadcasted_iota(jnp.int32, (t, t), 0) >= lax.broadcasted_iota(jnp.int32, (t, t), 1))
    gbs = gbs_ref[...]
    s_parts = []
    for g in range(GMLP_W // GMLP_GW):
        cols = slice(g * GMLP_GW, (g + 1) * GMLP_GW)
        wm = jnp.where(tri, gws_ref[g][0:t, 0:t], 0.0)
        s = jnp.broadcast_to(gbs[0:t, cols], (nb, t, GMLP_GW))
        for k in range(t):
            s = s + wm[:, k:k + 1] * vn[:, k:k + 1, cols]
        s_parts.append(s)
    y_gm = u * jnp.concatenate(s_parts, axis=2).reshape(rows, GMLP_W)

    t_pool = _dot(y_pool.astype(BF16), wbop_ref[...])
    t_lru = _dot(y_lru.astype(BF16), wbol_ref[...])
    y = _merge_out(lambda: hb, win_ref, t_pool, t_lru, y_gm, wbog_ref, wout_ref)
    o_ref[...] = x + gt_ref[...] * y.reshape(nb, t, D)


def _mix_sample(x, mods, l, wts, gws, gbs_rows, spool, sconv, sh0, *, nb):
    nbt, t, _ = x.shape
    grid = (nbt // nb, 1)
    xspec = pl.BlockSpec((nb, t, D), lambda i, j: (i, 0, 0))

    def state_spec(r):
        return pl.BlockSpec((None, nb, r, D), lambda i, j: (l, i, 0, 0))

    def out_state_spec(r):
        return pl.BlockSpec((nb, r, D), lambda i, j: (i, 0, 0))

    in_specs = ([xspec, _mod_spec(l, 3, nb), _mod_spec(l, 4, nb), _mod_spec(l, 5, nb)]
                + _mix_weight_specs(l)
                + [pl.BlockSpec((None, GMLP_W // GMLP_GW, CHUNK, CHUNK), lambda i, j: (l, 0, 0, 0)),
                   pl.BlockSpec((None, CHUNK, GMLP_W), lambda i, j: (l, 0, 0))]
                + _mix_proj_specs(l)
                + [state_spec(POOL_PREFIX), state_spec(CONV_W - 1), state_spec(1)])
    out_specs = [xspec, out_state_spec(POOL_PREFIX), out_state_spec(CONV_W - 1), out_state_spec(1),
                 pl.BlockSpec((nb, t, GMLP_W), lambda i, j: (i, 0, 0))]
    out_shape = [
        jax.ShapeDtypeStruct(x.shape, F32),
        jax.ShapeDtypeStruct((nbt, POOL_PREFIX, D), F32),
        jax.ShapeDtypeStruct((nbt, CONV_W - 1, D), F32),
        jax.ShapeDtypeStruct((nbt, 1, D), F32),
        jax.ShapeDtypeStruct((nbt, t, GMLP_W), F32),
    ]
    scratch = [
        pltpu.VMEM((nb, POOL_PREFIX + 1 + t, D), F32),
        pltpu.VMEM((nb, CPREF + t, D), F32),
    ]
    (norm, w_in, pool_w, pool_scale, conv_w, conv_b, wri, br, bi, lam, gnorm,
     wbop, wbol, wbog, wout) = wts
    return pl.pallas_call(
        functools.partial(_mix_sample_body, nb=nb, t=t),
        grid=grid,
        in_specs=in_specs,
        out_specs=out_specs,
        out_shape=out_shape,
        scratch_shapes=scratch,
        compiler_params=pltpu.CompilerParams(
            dimension_semantics=("arbitrary", "arbitrary"), vmem_limit_bytes=VMEM_LIMIT),
        name="mix_sample",
    )(x, mods, mods, mods, norm, w_in, pool_w, pool_scale, conv_w, conv_b, wri, br, bi, lam, gnorm,
      gws, gbs_rows, wbop, wbol, wbog, wout, spool, sconv, sh0)


def kernel(x_prompt, x_sample, c_prompt, c_sample, state_pool, state_conv, state_lru, w_ada, b_ada,
           norm_ffn1, ffn1_w_gate, ffn1_w_up, ffn1_w_down, norm_mix, w_in, pool_w, pool_scale,
           conv_w, conv_b, lru_wr, lru_br, lru_wi, lru_bi, lru_lambda, gmlp_norm, gmlp_ws, gmlp_bs,
           wbo_pool, wbo_lru, wbo_gmlp, w_out, norm_ffn2, ffn2_w_gate, ffn2_w_up, ffn2_w_down,
           norm_final):
    nbp, tp, _ = x_prompt.shape
    nbs, ts, _ = x_sample.shape
    tm_ffn, nb_ffn, tm_mix, nb_mix = 512, 64, 512, 32

    mods_p, mods_s = _ada(c_prompt, c_sample, w_ada, b_ada)

    def row(v):
        return v.reshape(DEPTH, 1, v.shape[-1])

    bf = lambda w: w.astype(BF16)
    f1 = (row(norm_ffn1), ffn1_w_gate, ffn1_w_up, ffn1_w_down)
    f2 = (row(norm_ffn2), ffn2_w_gate, ffn2_w_up, ffn2_w_down)
    wri = bf(0.5 * jnp.concatenate([lru_wr, lru_wi], axis=-1))
    col_scale = jnp.where(jnp.arange(IN_W) >= GATE_COL0, 0.5, 1.0).astype(F32)
    mixw = (row(norm_mix), bf(w_in * col_scale), bf(pool_w), row(pool_scale), conv_w, row(conv_b),
            wri, row(lru_br), row(lru_bi), row(lru_lambda), row(gmlp_norm),
            bf(0.5 * wbo_pool), bf(0.5 * wbo_lru), bf(0.5 * wbo_gmlp), bf(w_out))
    gbs_rows = jnp.repeat(jnp.swapaxes(gmlp_bs, 1, 2), GMLP_GW, axis=2)
    tperm = np.array([(r % SUBLANES) * SEG + r // SUBLANES for r in range(CHUNK)])
    gws_p = gmlp_ws[:, :, tperm][:, :, :, tperm]
    gbs_rows_p = gbs_rows[:, tperm, :]
    nf = norm_final.reshape(1, D)
    sh0 = state_lru.reshape(DEPTH, nbs, 1, D)

    xp = x_prompt.reshape(nbp, tp // CHUNK, SUBLANES, SEG, D).swapaxes(2, 3).reshape(nbp, tp, D)
    xs = x_sample
    pp, pc, ph, sp, sc, sh, sv = [], [], [], [], [], [], []
    for l in range(DEPTH):
        final = l == DEPTH - 1
        xp, xs = _ffn(xp, xs, mods_p, mods_s, l, 0, *f1, nf, tm=tm_ffn, nb=nb_ffn, final=False)
        xp, tailp, tailc, nh = _mix_prompt(xp, mods_p, l, mixw, gws_p, gbs_rows_p, tm=tm_mix)
        npool = tailp[:, SEG - POOL_PREFIX:]
        nconv = tailc[:, SEG - (CONV_W - 1):]
        pp.append(npool); pc.append(nconv); ph.append(nh.reshape(nbp, D))
        xs, npool, nconv, nh, nv = _mix_sample(xs, mods_s, l, mixw, gmlp_ws, gbs_rows,
                                               state_pool, state_conv, sh0, nb=nb_mix)
        xp, xs = _ffn(xp, xs, mods_p, mods_s, l, 6, *f2, nf, tm=tm_ffn, nb=nb_ffn, final=final)
        sp.append(npool); sc.append(nconv); sh.append(nh.reshape(nbs, D)); sv.append(nv)
    xp = xp.reshape(nbp, tp // CHUNK, SEG, SUBLANES, D).swapaxes(2, 3).reshape(nbp, tp, D)
    return (xp, xs, jnp.stack(pp), jnp.stack(pc), jnp.stack(ph),
            jnp.stack(sp), jnp.stack(sc), jnp.stack(sh), jnp.stack(sv))
```

```python
import functools

import numpy as np
import jax
import jax.numpy as jnp
from jax import lax
from jax.experimental import pallas as pl
from jax.experimental.pallas import tpu as pltpu

D = 1024
DEPTH = 4
PAST_LEN = 16384
POOL_WINDOWS = (2, 4, 8, 16)
POOL_GW = 256
POOL_PREFIX = 15
LRU_HEADS = 8
LRU_HD = 128
CONV_W = 4
LRU_C = 8.0
GMLP_W = 512
GMLP_GW = 128
CHUNK = 128
IN_W = 7168
D_FF = 2816
N_MOD = 9
EPS = 1e-6

GATE_COL0 = 3 * D + 2 * GMLP_W
LOG2E = 1.4426950408889634

FF_CHUNK = 256
N_FF_CHUNKS = D_FF // FF_CHUNK
NEXT_INPUT_PIECES = 8
SUBLANES = 8
LANES = 128
SEG = CHUNK // SUBLANES
CPREF = 8
VMEM_LIMIT = 56 * 1024 * 1024

F32 = jnp.float32
BF16 = jnp.bfloat16


def _dot(a, b):
    return jnp.dot(a, b, preferred_element_type=F32)


def _norm_mod(x, n, sc, sh):
    ms = jnp.mean(x * x, axis=-1, keepdims=True)
    y = x * lax.rsqrt(ms + EPS) * n
    return y * (1.0 + sc) + sh


def _ada_body(cp_ref, cs_ref, w_ref, b_ref, op_ref, os_ref):
    w = w_ref[...].astype(BF16)
    b = b_ref[...]

    def f(c):
        s = c * jax.nn.sigmoid(c)
        return _dot(s.astype(BF16), w) + b

    op_ref[...] = f(cp_ref[...])[:, None, :]
    os_ref[...] = f(cs_ref[...])[:, None, :]


def _ada(c_prompt, c_sample, w_ada, b_ada):
    nbp, nbs = c_prompt.shape[0], c_sample.shape[0]
    return pl.pallas_call(
        _ada_body,
        grid=(DEPTH, N_MOD),
        in_specs=[
            pl.BlockSpec((nbp, D), lambda l, j: (0, 0)),
            pl.BlockSpec((nbs, D), lambda l, j: (0, 0)),
            pl.BlockSpec((None, D, D), lambda l, j: (l, 0, j)),
            pl.BlockSpec((None, 1, D), lambda l, j: (l, 0, j)),
        ],
        out_specs=[
            pl.BlockSpec((None, None, nbp, 1, D), lambda l, j: (l, j, 0, 0, 0)),
            pl.BlockSpec((None, None, nbs, 1, D), lambda l, j: (l, j, 0, 0, 0)),
        ],
        out_shape=[
            jax.ShapeDtypeStruct((DEPTH, N_MOD, nbp, 1, D), F32),
            jax.ShapeDtypeStruct((DEPTH, N_MOD, nbs, 1, D), F32),
        ],
        name="ada",
    )(c_prompt, c_sample, w_ada, b_ada.reshape(DEPTH, 1, N_MOD * D))


def _ffn_input(x_ref, sh_ref, sc_ref, n_ref, rows=slice(None)):
    x = x_ref[:, rows, :]
    h = _norm_mod(x, n_ref[...], sc_ref[...], sh_ref[...])
    return h.reshape(x.shape[0] * x.shape[1], D).astype(BF16)


def _ffn_chunk(hb, wg, wu, wd):
    g = _dot(hb, wg)
    u = _dot(hb, wu)
    a = (g * jax.nn.sigmoid(g)) * u
    return _dot(a.astype(BF16), wd)


def _ffn_finish(x_ref, acc, gt_ref, nf_ref, o_ref, final):
    x = x_ref[...]
    out = x + (0.5 * gt_ref[...]) * acc.reshape(x.shape)
    if final:
        ms = jnp.mean(out * out, axis=-1, keepdims=True)
        out = out * lax.rsqrt(ms + EPS) * nf_ref[...]
    o_ref[...] = out


def _ffn_tile(x_ref, hb_get, gt_ref, wg_s, wu_s, wd_s, nf_ref, o_ref, final, chunk_hook=None):
    rows = x_ref.shape[0] * x_ref.shape[1]
    acc = jnp.zeros((rows, D), F32)
    for c in range(N_FF_CHUNKS):
        if chunk_hook is not None:
            chunk_hook(c)
        acc = acc + _ffn_chunk(hb_get(), wg_s[c], wu_s[c], wd_s[c])
    _ffn_finish(x_ref, acc, gt_ref, nf_ref, o_ref, final)


def _ffn_body(xp_ref, shp_ref, scp_ref, gtp_ref, xn_ref, shn_ref, scn_ref,
              xs_ref, shs_ref, scs_ref, gts_ref, n_ref,
              wg_ref, wu_ref, wd_ref, nf_ref, op_ref, os_ref, wg_s, wu_s, wd_s, hb_s, acc_s,
              *, n_p, final):
    i = pl.program_id(0)
    first_slot = 1 - N_FF_CHUNKS % 2

    @pl.when(i == 0)
    def _():
        hb_s[first_slot] = _ffn_input(xp_ref, shp_ref, scp_ref, n_ref)
        acc_s[...] = jnp.zeros(acc_s.shape, F32)

    @pl.when(i < N_FF_CHUNKS)
    def _():
        wg = wg_ref[...].astype(BF16)
        wu = wu_ref[...].astype(BF16)
        wd = wd_ref[...].astype(BF16)
        wg_s[i] = wg
        wu_s[i] = wu
        wd_s[i] = wd
        acc_s[...] += _ffn_chunk(hb_s[first_slot], wg, wu, wd)

    @pl.when(i == N_FF_CHUNKS - 1)
    def _():
        _ffn_finish(xp_ref, acc_s[...], gtp_ref, nf_ref, op_ref, final)
        hb_s[1 - first_slot] = _ffn_input(xn_ref, shn_ref, scn_ref, n_ref)

    @pl.when(jnp.logical_and(i >= N_FF_CHUNKS, i < N_FF_CHUNKS + n_p - 1))
    def _():
        slot = lax.rem(i, 2)

        tm = xp_ref.shape[1]
        piece = tm // NEXT_INPUT_PIECES

        def next_input(c):
            if c < NEXT_INPUT_PIECES:
                rows = slice(c * piece, (c + 1) * piece)
                hb_s[1 - slot, rows, :] = _ffn_input(xn_ref, shn_ref, scn_ref, n_ref, rows)

        _ffn_tile(xp_ref, lambda: hb_s[slot], gtp_ref, wg_s, wu_s, wd_s, nf_ref, op_ref, final,
                  chunk_hook=next_input)

    @pl.when(i >= N_FF_CHUNKS + n_p - 1)
    def _():
        hb = _ffn_input(xs_ref, shs_ref, scs_ref, n_ref)
        _ffn_tile(xs_ref, lambda: hb, gts_ref, wg_s, wu_s, wd_s, nf_ref, os_ref, final)


def _ffn(xp, xs, mods_p, mods_s, l, k0, norm, wg, wu, wd, norm_final, *, tm, nb, final):
    nbp, tp, _ = xp.shape
    nbs, ts, _ = xs.shape
    tpb = tp // tm
    n_p, n_s = nbp * tpb, nbs // nb

    def p_tile(i):
        return jnp.clip(i - (N_FF_CHUNKS - 1), 0, n_p - 1)

    def s_tile(i):
        return jnp.clip(i - (N_FF_CHUNKS + n_p - 1), 0, n_s - 1)

    def w_chunk(i):
        return jnp.minimum(i, N_FF_CHUNKS - 1)

    def modp(k, ahead=0):
        return pl.BlockSpec((None, None, 1, 1, D),
                            lambda i: (l, k, p_tile(i + ahead) // tpb, 0, 0))

    def mods(k):
        return pl.BlockSpec((None, None, nb, 1, D), lambda i: (l, k, s_tile(i), 0, 0))

    xp_spec = pl.BlockSpec((1, tm, D), lambda i: (p_tile(i) // tpb, p_tile(i) % tpb, 0))
    xn_spec = pl.BlockSpec((1, tm, D), lambda i: (p_tile(i + 1) // tpb, p_tile(i + 1) % tpb, 0))
    xs_spec = pl.BlockSpec((nb, ts, D), lambda i: (s_tile(i), 0, 0))
    w_col = pl.BlockSpec((None, D, FF_CHUNK), lambda i: (l, 0, w_chunk(i)))
    w_row = pl.BlockSpec((None, FF_CHUNK, D), lambda i: (l, w_chunk(i), 0))
    return pl.pallas_call(
        functools.partial(_ffn_body, n_p=n_p, final=final),
        grid=(N_FF_CHUNKS + n_p - 1 + n_s,),
        in_specs=[
            xp_spec, modp(k0), modp(k0 + 1), modp(k0 + 2),
            xn_spec, modp(k0, 1), modp(k0 + 1, 1),
            xs_spec, mods(k0), mods(k0 + 1), mods(k0 + 2),
            pl.BlockSpec((None, 1, D), lambda i: (l, 0, 0)),
            w_col, w_col, w_row,
            pl.BlockSpec((1, D), lambda i: (0, 0)),
        ],
        out_specs=[xp_spec, xs_spec],
        out_shape=[jax.ShapeDtypeStruct(xp.shape, F32), jax.ShapeDtypeStruct(xs.shape, F32)],
        scratch_shapes=[
            pltpu.VMEM((N_FF_CHUNKS, D, FF_CHUNK), BF16),
            pltpu.VMEM((N_FF_CHUNKS, D, FF_CHUNK), BF16),
            pltpu.VMEM((N_FF_CHUNKS, FF_CHUNK, D), BF16),
            pltpu.VMEM((2, tm, D), BF16),
            pltpu.VMEM((tm, D), F32),
        ],
        compiler_params=pltpu.CompilerParams(
            dimension_semantics=("arbitrary",), vmem_limit_bytes=VMEM_LIMIT),
        name="ffn_final" if final else "ffn",
    )(xp, mods_p, mods_p, mods_p, xp, mods_p, mods_p, xs, mods_s, mods_s, mods_s,
      norm, wg, wu, wd, norm_final)


def _mod_spec(l, k, nb):
    return pl.BlockSpec((None, None, nb, 1, D), lambda i, j: (l, k, i, 0, 0))


def _row_spec(l, width):
    return pl.BlockSpec((None, 1, width), lambda i, j: (l, 0, 0))


def _full_spec(l, shape):
    nd = len(shape)
    return pl.BlockSpec((None,) + tuple(shape), lambda i, j: (l,) + (0,) * nd,
                        pipeline_mode=pl.Buffered(1))


def _scan8(a, b):
    row = lax.broadcasted_iota(jnp.int32, a.shape, 0) & (SUBLANES - 1)
    for d in (1, 2, 4):
        m = row >= d
        a_s = jnp.where(m, pltpu.roll(a, d, 0), 1.0)
        b_s = jnp.where(m, pltpu.roll(b, d, 0), 0.0)
        b = a * b_s + b
        a = a * a_s
    return a, b


def _lru_coeffs(xc, wri_ref, br, bi, lam, head0=0):
    xcb = xc.astype(BF16)
    rs, is_ = [], []
    for hd in range(xc.shape[1] // LRU_HD):
        res = _dot(xcb[:, hd * LRU_HD:(hd + 1) * LRU_HD], wri_ref[head0 + hd])
        rs.append(res[:, :LRU_HD])
        is_.append(res[:, LRU_HD:])
    tr = jnp.tanh(jnp.concatenate(rs, axis=1) + 0.5 * br)
    ti = jnp.tanh(jnp.concatenate(is_, axis=1) + 0.5 * bi)
    nl = -lam
    softplus = jnp.maximum(nl, 0.0) + jnp.log1p(jnp.exp(-jnp.abs(nl)))
    c1 = (-0.5 * LRU_C * LOG2E) * softplus
    a = jnp.exp2(c1 * (tr + 1.0))
    y = 1.0 - a * a
    root = jnp.where(y > 0.0, y * lax.rsqrt(y), 0.0)
    bx = root * ((0.5 * ti + 0.5) * xc)
    return a, bx


def _pool_project(d_parts, poolw_ref, pscale):
    ys = [_dot(d.astype(BF16), poolw_ref[g]) for g, d in enumerate(d_parts)]
    return jnp.concatenate(ys, axis=1) * pscale


def _rms(v, g):
    ms = jnp.mean(v * v, axis=-1, keepdims=True)
    return v * lax.rsqrt(ms + EPS) * g


def _merge_out(hb_get, win_ref, t_pool, t_lru, y_gm, wbog_ref, wout_ref):
    def gate(k):
        c0 = GATE_COL0 + k * D
        return jnp.tanh(_dot(hb_get(), win_ref[:, c0:c0 + D])) + 1.0

    merged = gate(0) * t_pool
    merged = merged + gate(1) * t_lru
    merged = merged + gate(2) * _dot(y_gm.astype(BF16), wbog_ref[...])
    return _dot(merged.astype(BF16), wout_ref[...])


def _extend(x, carry_ref, e_ref, cols):
    tm, cdim = x.shape
    nc = tm // CHUNK
    xr = pltpu.roll(x.reshape(tm // SUBLANES, SUBLANES, cdim), 1, 1).reshape(tm, cdim)
    prev = carry_ref[:, cols]
    if nc > 1:
        prev = jnp.concatenate([prev, xr[:tm - CHUNK]], axis=0)
    sub0 = (lax.broadcasted_iota(jnp.int32, (tm, cdim), 0) & (SUBLANES - 1)) == 0
    e_ref[:, 0:CHUNK, cols] = jnp.where(sub0, prev, xr).reshape(nc, CHUNK, cdim)
    e_ref[:, CHUNK:2 * CHUNK, cols] = x.reshape(nc, CHUNK, cdim)
    carry_ref[:, cols] = xr[tm - CHUNK:]


def _shifted(e_ref, k, cols=slice(None)):
    v = e_ref[:, pl.ds((SEG - k) * SUBLANES, CHUNK), cols]
    return v.reshape(v.shape[0] * CHUNK, v.shape[2])


def _window_sum(e_ref, ta_ref, tb_ref, c0, w):
    src, cols = e_ref, slice(c0, c0 + POOL_GW)
    bufs = (ta_ref, tb_ref)
    k, level, lo = 1, 0, 0
    while True:
        last = 2 * k == w
        lo = lo + k
        start = SEG if last else lo
        n = (2 * SEG - start) * SUBLANES
        val = (src[:, pl.ds(start * SUBLANES, n), cols]
               + src[:, pl.ds((start - k) * SUBLANES, n), cols])
        if last:
            return val.reshape(val.shape[0] * CHUNK, POOL_GW)
        dst = bufs[level % 2]
        dst[:, pl.ds(start * SUBLANES, n), :] = val
        src, cols = dst, slice(0, POOL_GW)
        k, level = 2 * k, level + 1


def _stored_time(shape, axis):
    r = lax.broadcasted_iota(jnp.int32, shape, axis)
    return (r & (SUBLANES - 1)) * SEG + ((r >> 3) & (SEG - 1))


def _mix_prompt_body(x_ref, sh_ref, sc_ref, gt_ref, xn_ref, shn_ref, scn_ref,
                     nm_ref, win_ref, poolw_ref, pscale_ref,
                     convw_ref, convb_ref, wri_ref, br_ref, bi_ref, lam_ref, gnorm_ref, gws_ref,
                     gbs_ref, wbop_ref, wbol_ref, wbog_ref, wout_ref,
                     o_ref, tailp_ref, tailc_ref, nh_ref,
                     ep_ref, cp_ref, ta_ref, tb_ref, ec_ref, cc_ref, pa_ref, h_ref, hc_ref,
                     yg_ref, hb_s, *, tm):
    j = pl.program_id(1)
    nt = pl.num_programs(1)
    nc = tm // CHUNK
    step = pl.program_id(0) * nt + j
    slot = lax.rem(step, 2)

    @pl.when(j == 0)
    def _():
        cp_ref[...] = jnp.zeros((CHUNK, D), F32)
        cc_ref[...] = jnp.zeros((CHUNK, D), F32)
        hc_ref[...] = jnp.zeros((1, D), F32)

    @pl.when(step == 0)
    def _():
        hb_s[0] = _ffn_input(x_ref, sh_ref, sc_ref, nm_ref)

    x = x_ref[...]
    piece = tm // NEXT_INPUT_PIECES

    def hb_get():
        return hb_s[slot]

    def next_input(k):
        rows = slice(k * piece, (k + 1) * piece)
        hb_s[1 - slot, rows, :] = _ffn_input(xn_ref, shn_ref, scn_ref, nm_ref, rows)

    all_cols = slice(0, D)
    xp = _dot(hb_get(), win_ref[:, 0:D])
    next_input(0)
    _extend(xp, cp_ref, ep_ref, all_cols)
    row = lax.broadcasted_iota(jnp.int32, (tm, LANES), 0)
    pos = j * tm + (row & ~(CHUNK - 1)) + _stored_time((tm, LANES), 0)
    d_parts = []
    for g, w in enumerate(POOL_WINDOWS):
        s = _window_sum(ep_ref, ta_ref, tb_ref, g * POOL_GW, w)
        inv = 1.0 / jnp.minimum(w, pos + 1).astype(F32)
        inv = jnp.concatenate([inv] * (POOL_GW // LANES), axis=1)
        d_parts.append(s * inv - xp[:, g * POOL_GW:(g + 1) * POOL_GW])
    y_pool = _pool_project(d_parts, poolw_ref, pscale_ref[...])
    next_input(1)

    xl = _dot(hb_get(), win_ref[:, D:2 * D])
    next_input(2)
    _extend(xl, cc_ref, ec_ref, all_cols)
    cw = convw_ref[...]
    xc = convb_ref[...] + cw[CONV_W - 1:CONV_W, :] * xl
    for k in range(CONV_W - 1):
        xc = xc + cw[k:k + 1, :] * _shifted(ec_ref, CONV_W - 1 - k)

    tailp_ref[...] = xp[tm - CHUNK:, :].reshape(SEG, SUBLANES, D)[:, SUBLANES - 1, :]
    tailc_ref[...] = xl[tm - CHUNK:, :].reshape(SEG, SUBLANES, D)[:, SUBLANES - 1, :]

    a, bx = _lru_coeffs(xc, wri_ref, br_ref[...], bi_ref[...], lam_ref[...])
    next_input(3)
    a3 = a.reshape(nc, CHUNK, D)
    b3 = bx.reshape(nc, CHUNK, D)
    hq = b3[:, 0:SUBLANES, :]
    paq = a3[:, 0:SUBLANES, :]
    pa_ref[:, 0:SUBLANES, :] = paq
    h_ref[:, 0:SUBLANES, :] = hq
    for q in range(1, SEG):
        rows = slice(q * SUBLANES, (q + 1) * SUBLANES)
        hq = a3[:, rows, :] * hq + b3[:, rows, :]
        paq = a3[:, rows, :] * paq
        pa_ref[:, rows, :] = paq
        h_ref[:, rows, :] = hq
    g_state = hc_ref[...]
    entry = []
    for c in range(nc):
        rows_c = []
        for s in range(SUBLANES):
            rows_c.append(g_state)
            g_state = paq[c, s:s + 1, :] * g_state + hq[c, s:s + 1, :]
        entry.append(jnp.concatenate(rows_c, axis=0))
    entry = jnp.stack(entry, axis=0)
    hc_ref[...] = g_state
    nh_ref[...] = g_state

    hs = [pa_ref[:, q * SUBLANES:(q + 1) * SUBLANES, :] * entry
          + h_ref[:, q * SUBLANES:(q + 1) * SUBLANES, :] for q in range(SEG)]
    hs = jnp.concatenate(hs, axis=1).reshape(tm, D)
    next_input(4)
    gl = _dot(hb_get(), win_ref[:, 2 * D:3 * D])
    y_lru = hs * jax.nn.gelu(gl)
    t_pool = _dot(y_pool.astype(BF16), wbop_ref[...])
    t_lru = _dot(y_lru.astype(BF16), wbol_ref[...])

    next_input(5)
    u = _dot(hb_get(), win_ref[:, 3 * D:3 * D + GMLP_W])
    next_input(6)
    v = _dot(hb_get(), win_ref[:, 3 * D + GMLP_W:4 * D])
    vnb = _rms(v, gnorm_ref[...]).astype(BF16)
    tri = _stored_time((CHUNK, CHUNK), 0) >= _stored_time((CHUNK, CHUNK), 1)
    gbs = gbs_ref[...]
    for g in range(GMLP_W // GMLP_GW):
        wm = jnp.where(tri, gws_ref[g], 0.0).astype(BF16)
        cols = slice(g * GMLP_GW, (g + 1) * GMLP_GW)
        for c in range(nc):
            rows = slice(c * CHUNK, (c + 1) * CHUNK)
            s = _dot(wm, vnb[rows, cols]) + gbs[:, cols]
            yg_ref[rows, cols] = u[rows, cols] * s
    y_gm = yg_ref[...]
    next_input(7)

    y = _merge_out(hb_get, win_ref, t_pool, t_lru, y_gm, wbog_ref, wout_ref)
    o_ref[...] = x + gt_ref[...] * y.reshape(1, tm, D)


def _mix_weight_specs(l):
    return [
        _row_spec(l, D),
        _full_spec(l, (D, IN_W)),
        _full_spec(l, (len(POOL_WINDOWS), POOL_GW, POOL_GW)),
        _row_spec(l, D),
        pl.BlockSpec((None, CONV_W, D), lambda i, j: (l, 0, 0)),
        _row_spec(l, D),
        _full_spec(l, (LRU_HEADS, LRU_HD, 2 * LRU_HD)),
        _row_spec(l, D), _row_spec(l, D), _row_spec(l, D),
        _row_spec(l, GMLP_W),
    ]


def _mix_proj_specs(l):
    return [_full_spec(l, (D, D)), _full_spec(l, (D, D)), _full_spec(l, (GMLP_W, D)),
            _full_spec(l, (D, D))]


def _mix_prompt(x, mods, l, wts, gws, gbs_rows, *, tm):
    nbt, tt, _ = x.shape
    grid = (nbt, tt // tm)
    nt = tt // tm
    last = nbt * nt - 1
    xspec = pl.BlockSpec((1, tm, D), lambda i, j: (i, j, 0))

    def nxt(i, j):
        s2 = jnp.minimum(i * nt + j + 1, last)
        return s2 // nt, s2 % nt

    xn_spec = pl.BlockSpec((1, tm, D), lambda i, j: (*nxt(i, j), 0))

    def modn(k):
        return pl.BlockSpec((None, None, 1, 1, D), lambda i, j: (l, k, nxt(i, j)[0], 0, 0))

    in_specs = ([xspec, _mod_spec(l, 3, 1), _mod_spec(l, 4, 1), _mod_spec(l, 5, 1),
                 xn_spec, modn(3), modn(4)]
                + _mix_weight_specs(l)
                + [pl.BlockSpec((None, GMLP_W // GMLP_GW, CHUNK, CHUNK), lambda i, j: (l, 0, 0, 0)),
                   pl.BlockSpec((None, CHUNK, GMLP_W), lambda i, j: (l, 0, 0))]
                + _mix_proj_specs(l))
    nc = tm // CHUNK
    out_specs = [
        xspec,
        pl.BlockSpec((None, SEG, D), lambda i, j: (i, 0, 0)),
        pl.BlockSpec((None, SEG, D), lambda i, j: (i, 0, 0)),
        pl.BlockSpec((None, 1, D), lambda i, j: (i, 0, 0)),
    ]
    out_shape = [
        jax.ShapeDtypeStruct(x.shape, F32),
        jax.ShapeDtypeStruct((nbt, SEG, D), F32),
        jax.ShapeDtypeStruct((nbt, SEG, D), F32),
        jax.ShapeDtypeStruct((nbt, 1, D), F32),
    ]
    scratch = [
        pltpu.VMEM((nc, 2 * CHUNK, D), F32),
        pltpu.VMEM((CHUNK, D), F32),
        pltpu.VMEM((nc, 2 * CHUNK, POOL_GW), F32),
        pltpu.VMEM((nc, 2 * CHUNK, POOL_GW), F32),
        pltpu.VMEM((nc, 2 * CHUNK, D), F32),
        pltpu.VMEM((CHUNK, D), F32),
        pltpu.VMEM((nc, CHUNK, D), F32),
        pltpu.VMEM((nc, CHUNK, D), F32),
        pltpu.VMEM((1, D), F32),
        pltpu.VMEM((tm, GMLP_W), F32),
        pltpu.VMEM((2, tm, D), BF16),
    ]
    (norm, w_in, pool_w, pool_scale, conv_w, conv_b, wri, br, bi, lam, gnorm,
     wbop, wbol, wbog, wout) = wts
    return pl.pallas_call(
        functools.partial(_mix_prompt_body, tm=tm),
        grid=grid,
        in_specs=in_specs,
        out_specs=out_specs,
        out_shape=out_shape,
        scratch_shapes=scratch,
        compiler_params=pltpu.CompilerParams(
            dimension_semantics=("arbitrary", "arbitrary"), vmem_limit_bytes=VMEM_LIMIT),
        name="mix_prompt",
    )(x, mods, mods, mods, x, mods, mods, norm, w_in, pool_w, pool_scale, conv_w, conv_b, wri, br, bi, lam, gnorm,
      gws, gbs_rows, wbop, wbol, wbog, wout)


def _mix_sample_body(x_ref, sh_ref, sc_ref, gt_ref, nm_ref, win_ref, poolw_ref, pscale_ref,
                     convw_ref, convb_ref, wri_ref, br_ref, bi_ref, lam_ref, gnorm_ref, gws_ref,
                     gbs_ref, wbop_ref, wbol_ref, wbog_ref, wout_ref,
                     spool_ref, sconv_ref, sh0_ref,
                     o_ref, npool_ref, nconv_ref, nh_ref, vn_ref,
                     p_ref, c_ref, *, nb, t):
    rows = nb * t
    x = x_ref[...]
    h = _norm_mod(x, nm_ref[...], sc_ref[...], sh_ref[...])
    hb = h.reshape(rows, D).astype(BF16)

    xp = _dot(hb, win_ref[:, 0:D]).reshape(nb, t, D)
    p_ref[:, pl.ds(1, POOL_PREFIX), :] = spool_ref[...]
    p_ref[:, pl.ds(POOL_PREFIX + 1, t), :] = xp
    d_parts = []
    for g, w in enumerate(POOL_WINDOWS):
        cols = slice(g * POOL_GW, (g + 1) * POOL_GW)
        s = xp[:, :, cols]
        for k in range(1, w):
            s = s + p_ref[:, pl.ds(POOL_PREFIX + 1 - k, t), cols]
        d_parts.append((s / float(w) - xp[:, :, cols]).reshape(rows, POOL_GW))
    y_pool = _pool_project(d_parts, poolw_ref, pscale_ref[...])
    npool_ref[...] = p_ref[:, pl.ds(t + 1, POOL_PREFIX), :]

    xl = _dot(hb, win_ref[:, D:2 * D]).reshape(nb, t, D)
    c_ref[:, pl.ds(CPREF - (CONV_W - 1), CONV_W - 1), :] = sconv_ref[...]
    c_ref[:, pl.ds(CPREF, t), :] = xl
    cw = convw_ref[...]
    xc = convb_ref[...] + cw[3:4, :] * xl
    for k in range(CONV_W - 1):
        xc = xc + cw[k:k + 1, :] * c_ref[:, pl.ds(CPREF - (CONV_W - 1) + k, t), :]
    nconv_ref[...] = c_ref[:, pl.ds(CPREF + t - (CONV_W - 1), CONV_W - 1), :]

    a, bx = _lru_coeffs(xc.reshape(rows, D), wri_ref, br_ref[...], bi_ref[...], lam_ref[...])
    a, bx = _scan8(a, bx)
    hs = a.reshape(nb, t, D) * sh0_ref[...] + bx.reshape(nb, t, D)
    nh_ref[...] = hs[:, t - 1:t, :]
    gl = _dot(hb, win_ref[:, 2 * D:3 * D])
    y_lru = hs.reshape(rows, D) * jax.nn.gelu(gl)

    u = _dot(hb, win_ref[:, 3 * D:3 * D + GMLP_W])
    v = _dot(hb, win_ref[:, 3 * D + GMLP_W:4 * D])
    vn = _rms(v, gnorm_ref[...]).reshape(nb, t, GMLP_W)
    vn_ref[...] = vn
    tri = (lax.broadcasted_iota(jnp.int32, (t, t), 0) >= lax.broadcasted_iota(jnp.int32, (t, t), 1))
    gbs = gbs_ref[...]
    s_parts = []
    for g in range(GMLP_W // GMLP_GW):
        cols = slice(g * GMLP_GW, (g + 1) * GMLP_GW)
        wm = jnp.where(tri, gws_ref[g][0:t, 0:t], 0.0)
        s = jnp.broadcast_to(gbs[0:t, cols], (nb, t, GMLP_GW))
        for k in range(t):
            s = s + wm[:, k:k + 1] * vn[:, k:k + 1, cols]
        s_parts.append(s)
    y_gm = u * jnp.concatenate(s_parts, axis=2).reshape(rows, GMLP_W)

    t_pool = _dot(y_pool.astype(BF16), wbop_ref[...])
    t_lru = _dot(y_lru.astype(BF16), wbol_ref[...])
    y = _merge_out(lambda: hb, win_ref, t_pool, t_lru, y_gm, wbog_ref, wout_ref)
    o_ref[...] = x + gt_ref[...] * y.reshape(nb, t, D)


def _mix_sample(x, mods, l, wts, gws, gbs_rows, spool, sconv, sh0, *, nb):
    nbt, t, _ = x.shape
    grid = (nbt // nb, 1)
    xspec = pl.BlockSpec((nb, t, D), lambda i, j: (i, 0, 0))

    def state_spec(r):
        return pl.BlockSpec((None, nb, r, D), lambda i, j: (l, i, 0, 0))

    def out_state_spec(r):
        return pl.BlockSpec((nb, r, D), lambda i, j: (i, 0, 0))

    in_specs = ([xspec, _mod_spec(l, 3, nb), _mod_spec(l, 4, nb), _mod_spec(l, 5, nb)]
                + _mix_weight_specs(l)
                + [pl.BlockSpec((None, GMLP_W // GMLP_GW, CHUNK, CHUNK), lambda i, j: (l, 0, 0, 0)),
                   pl.BlockSpec((None, CHUNK, GMLP_W), lambda i, j: (l, 0, 0))]
                + _mix_proj_specs(l)
                + [state_spec(POOL_PREFIX), state_spec(CONV_W - 1), state_spec(1)])
    out_specs = [xspec, out_state_spec(POOL_PREFIX), out_state_spec(CONV_W - 1), out_state_spec(1),
                 pl.BlockSpec((nb, t, GMLP_W), lambda i, j: (i, 0, 0))]
    out_shape = [
        jax.ShapeDtypeStruct(x.shape, F32),
        jax.ShapeDtypeStruct((nbt, POOL_PREFIX, D), F32),
        jax.ShapeDtypeStruct((nbt, CONV_W - 1, D), F32),
        jax.ShapeDtypeStruct((nbt, 1, D), F32),
        jax.ShapeDtypeStruct((nbt, t, GMLP_W), F32),
    ]
    scratch = [
        pltpu.VMEM((nb, POOL_PREFIX + 1 + t, D), F32),
        pltpu.VMEM((nb, CPREF + t, D), F32),
    ]
    (norm, w_in, pool_w, pool_scale, conv_w, conv_b, wri, br, bi, lam, gnorm,
     wbop, wbol, wbog, wout) = wts
    return pl.pallas_call(
        functools.partial(_mix_sample_body, nb=nb, t=t),
        grid=grid,
        in_specs=in_specs,
        out_specs=out_specs,
        out_shape=out_shape,
        scratch_shapes=scratch,
        compiler_params=pltpu.CompilerParams(
            dimension_semantics=("arbitrary", "arbitrary"), vmem_limit_bytes=VMEM_LIMIT),
        name="mix_sample",
    )(x, mods, mods, mods, norm, w_in, pool_w, pool_scale, conv_w, conv_b, wri, br, bi, lam, gnorm,
      gws, gbs_rows, wbop, wbol, wbog, wout, spool, sconv, sh0)


def kernel(x_prompt, x_sample, c_prompt, c_sample, state_pool, state_conv, state_lru, w_ada, b_ada,
           norm_ffn1, ffn1_w_gate, ffn1_w_up, ffn1_w_down, norm_mix, w_in, pool_w, pool_scale,
           conv_w, conv_b, lru_wr, lru_br, lru_wi, lru_bi, lru_lambda, gmlp_norm, gmlp_ws, gmlp_bs,
           wbo_pool, wbo_lru, wbo_gmlp, w_out, norm_ffn2, ffn2_w_gate, ffn2_w_up, ffn2_w_down,
           norm_final):
    nbp, tp, _ = x_prompt.shape
    nbs, ts, _ = x_sample.shape
    tm_ffn, nb_ffn, tm_mix, nb_mix = 512, 64, 512, 32

    mods_p, mods_s = _ada(c_prompt, c_sample, w_ada, b_ada)

    def row(v):
        return v.reshape(DEPTH, 1, v.shape[-1])

    bf = lambda w: w.astype(BF16)
    f1 = (row(norm_ffn1), ffn1_w_gate, ffn1_w_up, ffn1_w_down)
    f2 = (row(norm_ffn2), ffn2_w_gate, ffn2_w_up, ffn2_w_down)
    wri = bf(0.5 * jnp.concatenate([lru_wr, lru_wi], axis=-1))
    col_scale = jnp.where(jnp.arange(IN_W) >= GATE_COL0, 0.5, 1.0).astype(F32)
    mixw = (row(norm_mix), bf(w_in * col_scale), bf(pool_w), row(pool_scale), conv_w, row(conv_b),
            wri, row(lru_br), row(lru_bi), row(lru_lambda), row(gmlp_norm),
            bf(0.5 * wbo_pool), bf(0.5 * wbo_lru), bf(0.5 * wbo_gmlp), bf(w_out))
    gbs_rows = jnp.repeat(jnp.swapaxes(gmlp_bs, 1, 2), GMLP_GW, axis=2)
    tperm = np.array([(r % SUBLANES) * SEG + r // SUBLANES for r in range(CHUNK)])
    gws_p = gmlp_ws[:, :, tperm][:, :, :, tperm]
    gbs_rows_p = gbs_rows[:, tperm, :]
    nf = norm_final.reshape(1, D)
    sh0 = state_lru.reshape(DEPTH, nbs, 1, D)

    xp = x_prompt.reshape(nbp, tp // CHUNK, SUBLANES, SEG, D).swapaxes(2, 3).reshape(nbp, tp, D)
    xs = x_sample
    pp, pc, ph, sp, sc, sh, sv = [], [], [], [], [], [], []
    for l in range(DEPTH):
        final = l == DEPTH - 1
        xp, xs = _ffn(xp, xs, mods_p, mods_s, l, 0, *f1, nf, tm=tm_ffn, nb=nb_ffn, final=False)
        xp, tailp, tailc, nh = _mix_prompt(xp, mods_p, l, mixw, gws_p, gbs_rows_p, tm=tm_mix)
        npool = tailp[:, SEG - POOL_PREFIX:]
        nconv = tailc[:, SEG - (CONV_W - 1):]
        pp.append(npool); pc.append(nconv); ph.append(nh.reshape(nbp, D))
        xs, npool, nconv, nh, nv = _mix_sample(xs, mods_s, l, mixw, gmlp_ws, gbs_rows,
                                               state_pool, state_conv, sh0, nb=nb_mix)
        xp, xs = _ffn(xp, xs, mods_p, mods_s, l, 6, *f2, nf, tm=tm_ffn, nb=nb_ffn, final=final)
        sp.append(npool); sc.append(nconv); sh.append(nh.reshape(nbs, D)); sv.append(nv)
    xp = xp.reshape(nbp, tp // CHUNK, SEG, SUBLANES, D).swapaxes(2, 3).reshape(nbp, tp, D)
    return (xp, xs, jnp.stack(pp), jnp.stack(pc), jnp.stack(ph),
            jnp.stack(sp), jnp.stack(sc), jnp.stack(sh), jnp.stack(sv))
```

```python
import functools

import numpy as np
import jax
import jax.numpy as jnp
from jax import lax
from jax.experimental import pallas as pl
from jax.experimental.pallas import tpu as pltpu

D = 1024
DEPTH = 4
PAST_LEN = 16384
POOL_WINDOWS = (2, 4, 8, 16)
POOL_GW = 256
POOL_PREFIX = 15
LRU_HEADS = 8
LRU_HD = 128
CONV_W = 4
LRU_C = 8.0
GMLP_W = 512
GMLP_GW = 128
CHUNK = 128
IN_W = 7168
D_FF = 2816
N_MOD = 9
EPS = 1e-6

GATE_COL0 = 3 * D + 2 * GMLP_W
LOG2E = 1.4426950408889634

FF_CHUNK = 256
N_FF_CHUNKS = D_FF // FF_CHUNK
NEXT_INPUT_PIECES = 8
SUBLANES = 8
LANES = 128
SEG = CHUNK // SUBLANES
CPREF = 8
VMEM_LIMIT = 56 * 1024 * 1024

F32 = jnp.float32
BF16 = jnp.bfloat16


def _dot(a, b):
    return jnp.dot(a, b, preferred_element_type=F32)


def _norm_mod(x, n, sc, sh):
    ms = jnp.mean(x * x, axis=-1, keepdims=True)
    y = x * lax.rsqrt(ms + EPS) * n
    return y * (1.0 + sc) + sh


def _ada_body(cp_ref, cs_ref, w_ref, b_ref, op_ref, os_ref):
    w = w_ref[...].astype(BF16)
    b = b_ref[...]

    def f(c):
        s = c * jax.nn.sigmoid(c)
        return _dot(s.astype(BF16), w) + b

    op_ref[...] = f(cp_ref[...])[:, None, :]
    os_ref[...] = f(cs_ref[...])[:, None, :]


def _ada(c_prompt, c_sample, w_ada, b_ada):
    nbp, nbs = c_prompt.shape[0], c_sample.shape[0]
    return pl.pallas_call(
        _ada_body,
        grid=(DEPTH, N_MOD),
        in_specs=[
            pl.BlockSpec((nbp, D), lambda l, j: (0, 0)),
            pl.BlockSpec((nbs, D), lambda l, j: (0, 0)),
            pl.BlockSpec((None, D, D), lambda l, j: (l, 0, j)),
            pl.BlockSpec((None, 1, D), lambda l, j: (l, 0, j)),
        ],
        out_specs=[
            pl.BlockSpec((None, None, nbp, 1, D), lambda l, j: (l, j, 0, 0, 0)),
            pl.BlockSpec((None, None, nbs, 1, D), lambda l, j: (l, j, 0, 0, 0)),
        ],
        out_shape=[
            jax.ShapeDtypeStruct((DEPTH, N_MOD, nbp, 1, D), F32),
            jax.ShapeDtypeStruct((DEPTH, N_MOD, nbs, 1, D), F32),
        ],
        name="ada",
    )(c_prompt, c_sample, w_ada, b_ada.reshape(DEPTH, 1, N_MOD * D))


def _ffn_input(x_ref, sh_ref, sc_ref, n_ref, rows=slice(None)):
    x = x_ref[:, rows, :]
    h = _norm_mod(x, n_ref[...], sc_ref[...], sh_ref[...])
    return h.reshape(x.shape[0] * x.shape[1], D).astype(BF16)


def _ffn_chunk(hb, wg, wu, wd):
    g = _dot(hb, wg)
    u = _dot(hb, wu)
    a = (g * jax.nn.sigmoid(g)) * u
    return _dot(a.astype(BF16), wd)


def _ffn_finish(x_ref, acc, gt_ref, nf_ref, o_ref, final):
    x = x_ref[...]
    out = x + (0.5 * gt_ref[...]) * acc.reshape(x.shape)
    if final:
        ms = jnp.mean(out * out, axis=-1, keepdims=True)
        out = out * lax.rsqrt(ms + EPS) * nf_ref[...]
    o_ref[...] = out


def _ffn_tile(x_ref, hb_get, gt_ref, wg_s, wu_s, wd_s, nf_ref, o_ref, final, chunk_hook=None):
    rows = x_ref.shape[0] * x_ref.shape[1]
    acc = jnp.zeros((rows, D), F32)
    for c in range(N_FF_CHUNKS):
        if chunk_hook is not None:
            chunk_hook(c)
        acc = acc + _ffn_chunk(hb_get(), wg_s[c], wu_s[c], wd_s[c])
    _ffn_finish(x_ref, acc, gt_ref, nf_ref, o_ref, final)


def _ffn_body(xp_ref, shp_ref, scp_ref, gtp_ref, xn_ref, shn_ref, scn_ref,
              xs_ref, shs_ref, scs_ref, gts_ref, n_ref,
              wg_ref, wu_ref, wd_ref, nf_ref, op_ref, os_ref, wg_s, wu_s, wd_s, hb_s, acc_s,
              *, n_p, final):
    i = pl.program_id(0)
    first_slot = 1 - N_FF_CHUNKS % 2

    @pl.when(i == 0)
    def _():
        hb_s[first_slot] = _ffn_input(xp_ref, shp_ref, scp_ref, n_ref)
        acc_s[...] = jnp.zeros(acc_s.shape, F32)

    @pl.when(i < N_FF_CHUNKS)
    def _():
        wg = wg_ref[...].astype(BF16)
        wu = wu_ref[...].astype(BF16)
        wd = wd_ref[...].astype(BF16)
        wg_s[i] = wg
        wu_s[i] = wu
        wd_s[i] = wd
        acc_s[...] += _ffn_chunk(hb_s[first_slot], wg, wu, wd)

    @pl.when(i == N_FF_CHUNKS - 1)
    def _():
        _ffn_finish(xp_ref, acc_s[...], gtp_ref, nf_ref, op_ref, final)
        hb_s[1 - first_slot] = _ffn_input(xn_ref, shn_ref, scn_ref, n_ref)

    @pl.when(jnp.logical_and(i >= N_FF_CHUNKS, i < N_FF_CHUNKS + n_p - 1))
    def _():
        slot = lax.rem(i, 2)

        tm = xp_ref.shape[1]
        piece = tm // NEXT_INPUT_PIECES

        def next_input(c):
            if c < NEXT_INPUT_PIECES:
                rows = slice(c * piece, (c + 1) * piece)
                hb_s[1 - slot, rows, :] = _ffn_input(xn_ref, shn_ref, scn_ref, n_ref, rows)

        _ffn_tile(xp_ref, lambda: hb_s[slot], gtp_ref, wg_s, wu_s, wd_s, nf_ref, op_ref, final,
                  chunk_hook=next_input)

    @pl.when(i >= N_FF_CHUNKS + n_p - 1)
    def _():
        hb = _ffn_input(xs_ref, shs_ref, scs_ref, n_ref)
        _ffn_tile(xs_ref, lambda: hb, gts_ref, wg_s, wu_s, wd_s, nf_ref, os_ref, final)


def _ffn(xp, xs, mods_p, mods_s, l, k0, norm, wg, wu, wd, norm_final, *, tm, nb, final):
    nbp, tp, _ = xp.shape
    nbs, ts, _ = xs.shape
    tpb = tp // tm
    n_p, n_s = nbp * tpb, nbs // nb

    def p_tile(i):
        return jnp.clip(i - (N_FF_CHUNKS - 1), 0, n_p - 1)

    def s_tile(i):
        return jnp.clip(i - (N_FF_CHUNKS + n_p - 1), 0, n_s - 1)

    def w_chunk(i):
        return jnp.minimum(i, N_FF_CHUNKS - 1)

    def modp(k, ahead=0):
        return pl.BlockSpec((None, None, 1, 1, D),
                            lambda i: (l, k, p_tile(i + ahead) // tpb, 0, 0))

    def mods(k):
        return pl.BlockSpec((None, None, nb, 1, D), lambda i: (l, k, s_tile(i), 0, 0))

    xp_spec = pl.BlockSpec((1, tm, D), lambda i: (p_tile(i) // tpb, p_tile(i) % tpb, 0))
    xn_spec = pl.BlockSpec((1, tm, D), lambda i: (p_tile(i + 1) // tpb, p_tile(i + 1) % tpb, 0))
    xs_spec = pl.BlockSpec((nb, ts, D), lambda i: (s_tile(i), 0, 0))
    w_col = pl.BlockSpec((None, D, FF_CHUNK), lambda i: (l, 0, w_chunk(i)))
    w_row = pl.BlockSpec((None, FF_CHUNK, D), lambda i: (l, w_chunk(i), 0))
    return pl.pallas_call(
        functools.partial(_ffn_body, n_p=n_p, final=final),
        grid=(N_FF_CHUNKS + n_p - 1 + n_s,),
        in_specs=[
            xp_spec, modp(k0), modp(k0 + 1), modp(k0 + 2),
            xn_spec, modp(k0, 1), modp(k0 + 1, 1),
            xs_spec, mods(k0), mods(k0 + 1), mods(k0 + 2),
            pl.BlockSpec((None, 1, D), lambda i: (l, 0, 0)),
            w_col, w_col, w_row,
            pl.BlockSpec((1, D), lambda i: (0, 0)),
        ],
        out_specs=[xp_spec, xs_spec],
        out_shape=[jax.ShapeDtypeStruct(xp.shape, F32), jax.ShapeDtypeStruct(xs.shape, F32)],
        scratch_shapes=[
            pltpu.VMEM((N_FF_CHUNKS, D, FF_CHUNK), BF16),
            pltpu.VMEM((N_FF_CHUNKS, D, FF_CHUNK), BF16),
            pltpu.VMEM((N_FF_CHUNKS, FF_CHUNK, D), BF16),
            pltpu.VMEM((2, tm, D), BF16),
            pltpu.VMEM((tm, D), F32),
        ],
        compiler_params=pltpu.CompilerParams(
            dimension_semantics=("arbitrary",), vmem_limit_bytes=VMEM_LIMIT),
        name="ffn_final" if final else "ffn",
    )(xp, mods_p, mods_p, mods_p, xp, mods_p, mods_p, xs, mods_s, mods_s, mods_s,
      norm, wg, wu, wd, norm_final)


def _mod_spec(l, k, nb):
    return pl.BlockSpec((None, None, nb, 1, D), lambda i, j: (l, k, i, 0, 0))


def _row_spec(l, width):
    return pl.BlockSpec((None, 1, width), lambda i, j: (l, 0, 0))


def _full_spec(l, shape):
    nd = len(shape)
    return pl.BlockSpec((None,) + tuple(shape), lambda i, j: (l,) + (0,) * nd,
                        pipeline_mode=pl.Buffered(1))


def _scan8(a, b):
    row = lax.broadcasted_iota(jnp.int32, a.shape, 0) & (SUBLANES - 1)
    for d in (1, 2, 4):
        m = row >= d
        a_s = jnp.where(m, pltpu.roll(a, d, 0), 1.0)
        b_s = jnp.where(m, pltpu.roll(b, d, 0), 0.0)
        b = a * b_s + b
        a = a * a_s
    return a, b


def _lru_coeffs(xc, wri_ref, br, bi, lam, head0=0):
    xcb = xc.astype(BF16)
    rs, is_ = [], []
    for hd in range(xc.shape[1] // LRU_HD):
        res = _dot(xcb[:, hd * LRU_HD:(hd + 1) * LRU_HD], wri_ref[head0 + hd])
        rs.append(res[:, :LRU_HD])
        is_.append(res[:, LRU_HD:])
    tr = jnp.tanh(jnp.concatenate(rs, axis=1) + 0.5 * br)
    ti = jnp.tanh(jnp.concatenate(is_, axis=1) + 0.5 * bi)
    nl = -lam
    softplus = jnp.maximum(nl, 0.0) + jnp.log1p(jnp.exp(-jnp.abs(nl)))
    c1 = (-0.5 * LRU_C * LOG2E) * softplus
    a = jnp.exp2(c1 * (tr + 1.0))
    y = 1.0 - a * a
    root = jnp.where(y > 0.0, y * lax.rsqrt(y), 0.0)
    bx = root * ((0.5 * ti + 0.5) * xc)
    return a, bx


def _pool_project(d_parts, poolw_ref, pscale):
    ys = [_dot(d.astype(BF16), poolw_ref[g]) for g, d in enumerate(d_parts)]
    return jnp.concatenate(ys, axis=1) * pscale


def _rms(v, g):
    ms = jnp.mean(v * v, axis=-1, keepdims=True)
    return v * lax.rsqrt(ms + EPS) * g


def _merge_out(hb_get, win_ref, t_pool, t_lru, y_gm, wbog_ref, wout_ref):
    def gate(k):
        c0 = GATE_COL0 + k * D
        return jnp.tanh(_dot(hb_get(), win_ref[:, c0:c0 + D])) + 1.0

    merged = gate(0) * t_pool
    merged = merged + gate(1) * t_lru
    merged = merged + gate(2) * _dot(y_gm.astype(BF16), wbog_ref[...])
    return _dot(merged.astype(BF16), wout_ref[...])


def _extend(x, carry_ref, e_ref, cols):
    tm, cdim = x.shape
    nc = tm // CHUNK
    xr = pltpu.roll(x.reshape(tm // SUBLANES, SUBLANES, cdim), 1, 1).reshape(tm, cdim)
    prev = carry_ref[:, cols]
    if nc > 1:
        prev = jnp.concatenate([prev, xr[:tm - CHUNK]], axis=0)
    sub0 = (lax.broadcasted_iota(jnp.int32, (tm, cdim), 0) & (SUBLANES - 1)) == 0
    e_ref[:, 0:CHUNK, cols] = jnp.where(sub0, prev, xr).reshape(nc, CHUNK, cdim)
    e_ref[:, CHUNK:2 * CHUNK, cols] = x.reshape(nc, CHUNK, cdim)
    carry_ref[:, cols] = xr[tm - CHUNK:]


def _shifted(e_ref, k, cols=slice(None)):
    v = e_ref[:, pl.ds((SEG - k) * SUBLANES, CHUNK), cols]
    return v.reshape(v.shape[0] * CHUNK, v.shape[2])


def _window_sum(e_ref, ta_ref, tb_ref, c0, w):
    src, cols = e_ref, slice(c0, c0 + POOL_GW)
    bufs = (ta_ref, tb_ref)
    k, level, lo = 1, 0, 0
    while True:
        last = 2 * k == w
        lo = lo + k
        start = SEG if last else lo
        n = (2 * SEG - start) * SUBLANES
        val = (src[:, pl.ds(start * SUBLANES, n), cols]
               + src[:, pl.ds((start - k) * SUBLANES, n), cols])
        if last:
            return val.reshape(val.shape[0] * CHUNK, POOL_GW)
        dst = bufs[level % 2]
        dst[:, pl.ds(start * SUBLANES, n), :] = val
        src, cols = dst, slice(0, POOL_GW)
        k, level = 2 * k, level + 1


def _stored_time(shape, axis):
    r = lax.broadcasted_iota(jnp.int32, shape, axis)
    return (r & (SUBLANES - 1)) * SEG + ((r >> 3) & (SEG - 1))


def _mix_prompt_body(x_ref, sh_ref, sc_ref, gt_ref, xn_ref, shn_ref, scn_ref,
                     nm_ref, win_ref, poolw_ref, pscale_ref,
                     convw_ref, convb_ref, wri_ref, br_ref, bi_ref, lam_ref, gnorm_ref, gws_ref,
                     gbs_ref, wbop_ref, wbol_ref, wbog_ref, wout_ref,
                     o_ref, tailp_ref, tailc_ref, nh_ref,
                     ep_ref, cp_ref, ta_ref, tb_ref, ec_ref, cc_ref, pa_ref, h_ref, hc_ref,
                     yg_ref, hb_s, *, tm):
    j = pl.program_id(1)
    nt = pl.num_programs(1)
    nc = tm // CHUNK
    step = pl.program_id(0) * nt + j
    slot = lax.rem(step, 2)

    @pl.when(j == 0)
    def _():
        cp_ref[...] = jnp.zeros((CHUNK, D), F32)
        cc_ref[...] = jnp.zeros((CHUNK, D), F32)
        hc_ref[...] = jnp.zeros((1, D), F32)

    @pl.when(step == 0)
    def _():
        hb_s[0] = _ffn_input(x_ref, sh_ref, sc_ref, nm_ref)

    x = x_ref[...]
    piece = tm // NEXT_INPUT_PIECES

    def hb_get():
        return hb_s[slot]

    def next_input(k):
        rows = slice(k * piece, (k + 1) * piece)
        hb_s[1 - slot, rows, :] = _ffn_input(xn_ref, shn_ref, scn_ref, nm_ref, rows)

    all_cols = slice(0, D)
    xp = _dot(hb_get(), win_ref[:, 0:D])
    next_input(0)
    _extend(xp, cp_ref, ep_ref, all_cols)
    row = lax.broadcasted_iota(jnp.int32, (tm, LANES), 0)
    pos = j * tm + (row & ~(CHUNK - 1)) + _stored_time((tm, LANES), 0)
    d_parts = []
    for g, w in enumerate(POOL_WINDOWS):
        s = _window_sum(ep_ref, ta_ref, tb_ref, g * POOL_GW, w)
        inv = 1.0 / jnp.minimum(w, pos + 1).astype(F32)
        inv = jnp.concatenate([inv] * (POOL_GW // LANES), axis=1)
        d_parts.append(s * inv - xp[:, g * POOL_GW:(g + 1) * POOL_GW])
    y_pool = _pool_project(d_parts, poolw_ref, pscale_ref[...])
    next_input(1)

    xl = _dot(hb_get(), win_ref[:, D:2 * D])
    next_input(2)
    _extend(xl, cc_ref, ec_ref, all_cols)
    cw = convw_ref[...]
    xc = convb_ref[...] + cw[CONV_W - 1:CONV_W, :] * xl
    for k in range(CONV_W - 1):
        xc = xc + cw[k:k + 1, :] * _shifted(ec_ref, CONV_W - 1 - k)

    tailp_ref[...] = xp[tm - CHUNK:, :].reshape(SEG, SUBLANES, D)[:, SUBLANES - 1, :]
    tailc_ref[...] = xl[tm - CHUNK:, :].reshape(SEG, SUBLANES, D)[:, SUBLANES - 1, :]

    a, bx = _lru_coeffs(xc, wri_ref, br_ref[...], bi_ref[...], lam_ref[...])
    next_input(3)
    a3 = a.reshape(nc, CHUNK, D)
    b3 = bx.reshape(nc, CHUNK, D)
    hq = b3[:, 0:SUBLANES, :]
    paq = a3[:, 0:SUBLANES, :]
    pa_ref[:, 0:SUBLANES, :] = paq
    h_ref[:, 0:SUBLANES, :] = hq
    for q in range(1, SEG):
        rows = slice(q * SUBLANES, (q + 1) * SUBLANES)
        hq = a3[:, rows, :] * hq + b3[:, rows, :]
        paq = a3[:, rows, :] * paq
        pa_ref[:, rows, :] = paq
        h_ref[:, rows, :] = hq
    g_state = hc_ref[...]
    entry = []
    for c in range(nc):
        rows_c = []
        for s in range(SUBLANES):
            rows_c.append(g_state)
            g_state = paq[c, s:s + 1, :] * g_state + hq[c, s:s + 1, :]
        entry.append(jnp.concatenate(rows_c, axis=0))
    entry = jnp.stack(entry, axis=0)
    hc_ref[...] = g_state
    nh_ref[...] = g_state

    hs = [pa_ref[:, q * SUBLANES:(q + 1) * SUBLANES, :] * entry
          + h_ref[:, q * SUBLANES:(q + 1) * SUBLANES, :] for q in range(SEG)]
    hs = jnp.concatenate(hs, axis=1).reshape(tm, D)
    next_input(4)
    gl = _dot(hb_get(), win_ref[:, 2 * D:3 * D])
    y_lru = hs * jax.nn.gelu(gl)
    t_pool = _dot(y_pool.astype(BF16), wbop_ref[...])
    t_lru = _dot(y_lru.astype(BF16), wbol_ref[...])

    next_input(5)
    u = _dot(hb_get(), win_ref[:, 3 * D:3 * D + GMLP_W])
    next_input(6)
    v = _dot(hb_get(), win_ref[:, 3 * D + GMLP_W:4 * D])
    vnb = _rms(v, gnorm_ref[...]).astype(BF16)
    tri = _stored_time((CHUNK, CHUNK), 0) >= _stored_time((CHUNK, CHUNK), 1)
    gbs = gbs_ref[...]
    for g in range(GMLP_W // GMLP_GW):
        wm = jnp.where(tri, gws_ref[g], 0.0).astype(BF16)
        cols = slice(g * GMLP_GW, (g + 1) * GMLP_GW)
        for c in range(nc):
            rows = slice(c * CHUNK, (c + 1) * CHUNK)
            s = _dot(wm, vnb[rows, cols]) + gbs[:, cols]
            yg_ref[rows, cols] = u[rows, cols] * s
    y_gm = yg_ref[...]
    next_input(7)

    y = _merge_out(hb_get, win_ref, t_pool, t_lru, y_gm, wbog_ref, wout_ref)
    o_ref[...] = x + gt_ref[...] * y.reshape(1, tm, D)


def _mix_weight_specs(l):
    return [
        _row_spec(l, D),
        _full_spec(l, (D, IN_W)),
        _full_spec(l, (len(POOL_WINDOWS), POOL_GW, POOL_GW)),
        _row_spec(l, D),
        pl.BlockSpec((None, CONV_W, D), lambda i, j: (l, 0, 0)),
        _row_spec(l, D),
        _full_spec(l, (LRU_HEADS, LRU_HD, 2 * LRU_HD)),
        _row_spec(l, D), _row_spec(l, D), _row_spec(l, D),
        _row_spec(l, GMLP_W),
    ]


def _mix_proj_specs(l):
    return [_full_spec(l, (D, D)), _full_spec(l, (D, D)), _full_spec(l, (GMLP_W, D)),
            _full_spec(l, (D, D))]


def _mix_prompt(x, mods, l, wts, gws, gbs_rows, *, tm):
    nbt, tt, _ = x.shape
    grid = (nbt, tt // tm)
    nt = tt // tm
    last = nbt * nt - 1
    xspec = pl.BlockSpec((1, tm, D), lambda i, j: (i, j, 0))

    def nxt(i, j):
        s2 = jnp.minimum(i * nt + j + 1, last)
        return s2 // nt, s2 % nt

    xn_spec = pl.BlockSpec((1, tm, D), lambda i, j: (*nxt(i, j), 0))

    def modn(k):
        return pl.BlockSpec((None, None, 1, 1, D), lambda i, j: (l, k, nxt(i, j)[0], 0, 0))

    in_specs = ([xspec, _mod_spec(l, 3, 1), _mod_spec(l, 4, 1), _mod_spec(l, 5, 1),
                 xn_spec, modn(3), modn(4)]
                + _mix_weight_specs(l)
                + [pl.BlockSpec((None, GMLP_W // GMLP_GW, CHUNK, CHUNK), lambda i, j: (l, 0, 0, 0)),
                   pl.BlockSpec((None, CHUNK, GMLP_W), lambda i, j: (l, 0, 0))]
                + _mix_proj_specs(l))
    nc = tm // CHUNK
    out_specs = [
        xspec,
        pl.BlockSpec((None, SEG, D), lambda i, j: (i, 0, 0)),
        pl.BlockSpec((None, SEG, D), lambda i, j: (i, 0, 0)),
        pl.BlockSpec((None, 1, D), lambda i, j: (i, 0, 0)),
    ]
    out_shape = [
        jax.ShapeDtypeStruct(x.shape, F32),
        jax.ShapeDtypeStruct((nbt, SEG, D), F32),
        jax.ShapeDtypeStruct((nbt, SEG, D), F32),
        jax.ShapeDtypeStruct((nbt, 1, D), F32),
    ]
    scratch = [
        pltpu.VMEM((nc, 2 * CHUNK, D), F32),
        pltpu.VMEM((CHUNK, D), F32),
        pltpu.VMEM((nc, 2 * CHUNK, POOL_GW), F32),
        pltpu.VMEM((nc, 2 * CHUNK, POOL_GW), F32),
        pltpu.VMEM((nc, 2 * CHUNK, D), F32),
        pltpu.VMEM((CHUNK, D), F32),
        pltpu.VMEM((nc, CHUNK, D), F32),
        pltpu.VMEM((nc, CHUNK, D), F32),
        pltpu.VMEM((1, D), F32),
        pltpu.VMEM((tm, GMLP_W), F32),
        pltpu.VMEM((2, tm, D), BF16),
    ]
    (norm, w_in, pool_w, pool_scale, conv_w, conv_b, wri, br, bi, lam, gnorm,
     wbop, wbol, wbog, wout) = wts
    return pl.pallas_call(
        functools.partial(_mix_prompt_body, tm=tm),
        grid=grid,
        in_specs=in_specs,
        out_specs=out_specs,
        out_shape=out_shape,
        scratch_shapes=scratch,
        compiler_params=pltpu.CompilerParams(
            dimension_semantics=("arbitrary", "arbitrary"), vmem_limit_bytes=VMEM_LIMIT),
        name="mix_prompt",
    )(x, mods, mods, mods, x, mods, mods, norm, w_in, pool_w, pool_scale, conv_w, conv_b, wri, br, bi, lam, gnorm,
      gws, gbs_rows, wbop, wbol, wbog, wout)


def _mix_sample_body(x_ref, sh_ref, sc_ref, gt_ref, nm_ref, win_ref, poolw_ref, pscale_ref,
                     convw_ref, convb_ref, wri_ref, br_ref, bi_ref, lam_ref, gnorm_ref, gws_ref,
                     gbs_ref, wbop_ref, wbol_ref, wbog_ref, wout_ref,
                     spool_ref, sconv_ref, sh0_ref, _pool_acc, _conv_acc, _h_acc, _vn_acc,
                     o_ref, npool_ref, nconv_ref, nh_ref, vn_ref,
                     p_ref, c_ref, *, nb, t):
    rows = nb * t
    x = x_ref[...]
    h = _norm_mod(x, nm_ref[...], sc_ref[...], sh_ref[...])
    hb = h.reshape(rows, D).astype(BF16)

    xp = _dot(hb, win_ref[:, 0:D]).reshape(nb, t, D)
    p_ref[:, pl.ds(1, POOL_PREFIX), :] = spool_ref[...]
    p_ref[:, pl.ds(POOL_PREFIX + 1, t), :] = xp
    d_parts = []
    for g, w in enumerate(POOL_WINDOWS):
        cols = slice(g * POOL_GW, (g + 1) * POOL_GW)
        s = xp[:, :, cols]
        for k in range(1, w):
            s = s + p_ref[:, pl.ds(POOL_PREFIX + 1 - k, t), cols]
        d_parts.append((s / float(w) - xp[:, :, cols]).reshape(rows, POOL_GW))
    y_pool = _pool_project(d_parts, poolw_ref, pscale_ref[...])
    npool_ref[...] = p_ref[:, pl.ds(t + 1, POOL_PREFIX), :]

    xl = _dot(hb, win_ref[:, D:2 * D]).reshape(nb, t, D)
    c_ref[:, pl.ds(CPREF - (CONV_W - 1), CONV_W - 1), :] = sconv_ref[...]
    c_ref[:, pl.ds(CPREF, t), :] = xl
    cw = convw_ref[...]
    xc = convb_ref[...] + cw[3:4, :] * xl
    for k in range(CONV_W - 1):
        xc = xc + cw[k:k + 1, :] * c_ref[:, pl.ds(CPREF - (CONV_W - 1) + k, t), :]
    nconv_ref[...] = c_ref[:, pl.ds(CPREF + t - (CONV_W - 1), CONV_W - 1), :]

    a, bx = _lru_coeffs(xc.reshape(rows, D), wri_ref, br_ref[...], bi_ref[...], lam_ref[...])
    a, bx = _scan8(a, bx)
    hs = a.reshape(nb, t, D) * sh0_ref[...] + bx.reshape(nb, t, D)
    nh_ref[...] = hs[:, t - 1:t, :]
    gl = _dot(hb, win_ref[:, 2 * D:3 * D])
    y_lru = hs.reshape(rows, D) * jax.nn.gelu(gl)

    u = _dot(hb, win_ref[:, 3 * D:3 * D + GMLP_W])
    v = _dot(hb, win_ref[:, 3 * D + GMLP_W:4 * D])
    vn = _rms(v, gnorm_ref[...]).reshape(nb, t, GMLP_W)
    vn_ref[...] = vn
    tri = (lax.broadcasted_iota(jnp.int32, (t, t), 0) >= lax.broadcasted_iota(jnp.int32, (t, t), 1))
    gbs = gbs_ref[...]
    s_parts = []
    for g in range(GMLP_W // GMLP_GW):
        cols = slice(g * GMLP_GW, (g + 1) * GMLP_GW)
        wm = jnp.where(tri, gws_ref[g][0:t, 0:t], 0.0)
        s = jnp.broadcast_to(gbs[0:t, cols], (nb, t, GMLP_GW))
        for k in range(t):
            s = s + wm[:, k:k + 1] * vn[:, k:k + 1, cols]
        s_parts.append(s)
    y_gm = u * jnp.concatenate(s_parts, axis=2).reshape(rows, GMLP_W)

    t_pool = _dot(y_pool.astype(BF16), wbop_ref[...])
    t_lru = _dot(y_lru.astype(BF16), wbol_ref[...])
    y = _merge_out(lambda: hb, win_ref, t_pool, t_lru, y_gm, wbog_ref, wout_ref)
    o_ref[...] = x + gt_ref[...] * y.reshape(nb, t, D)


def _mix_sample(x, mods, l, wts, gws, gbs_rows, spool, sconv, sh0, acc, *, nb):
    nbt, t, _ = x.shape
    grid = (nbt // nb, 1)
    xspec = pl.BlockSpec((nb, t, D), lambda i, j: (i, 0, 0))

    def state_spec(r):
        return pl.BlockSpec((None, nb, r, D), lambda i, j: (l, i, 0, 0))

    def out_state_spec(r, width=D):
        return pl.BlockSpec((None, nb, r, width), lambda i, j: (l, i, 0, 0))

    n_in = 24

    in_specs = ([xspec, _mod_spec(l, 3, nb), _mod_spec(l, 4, nb), _mod_spec(l, 5, nb)]
                + _mix_weight_specs(l)
                + [pl.BlockSpec((None, GMLP_W // GMLP_GW, CHUNK, CHUNK), lambda i, j: (l, 0, 0, 0)),
                   pl.BlockSpec((None, CHUNK, GMLP_W), lambda i, j: (l, 0, 0))]
                + _mix_proj_specs(l)
                + [state_spec(POOL_PREFIX), state_spec(CONV_W - 1), state_spec(1)]
                + [pl.BlockSpec(memory_space=pl.ANY)] * len(acc))
    assert len(in_specs) == n_in + len(acc)
    out_specs = [xspec, out_state_spec(POOL_PREFIX), out_state_spec(CONV_W - 1), out_state_spec(1),
                 out_state_spec(t, GMLP_W)]
    out_shape = [jax.ShapeDtypeStruct(x.shape, F32)] + [
        jax.ShapeDtypeStruct(a.shape, F32) for a in acc]
    scratch = [
        pltpu.VMEM((nb, POOL_PREFIX + 1 + t, D), F32),
        pltpu.VMEM((nb, CPREF + t, D), F32),
    ]
    (norm, w_in, pool_w, pool_scale, conv_w, conv_b, wri, br, bi, lam, gnorm,
     wbop, wbol, wbog, wout) = wts
    return pl.pallas_call(
        functools.partial(_mix_sample_body, nb=nb, t=t),
        grid=grid,
        in_specs=in_specs,
        out_specs=out_specs,
        out_shape=out_shape,
        scratch_shapes=scratch,
        input_output_aliases={n_in + k: 1 + k for k in range(len(acc))},
        compiler_params=pltpu.CompilerParams(
            dimension_semantics=("arbitrary", "arbitrary"), vmem_limit_bytes=VMEM_LIMIT),
        name="mix_sample",
    )(x, mods, mods, mods, norm, w_in, pool_w, pool_scale, conv_w, conv_b, wri, br, bi, lam, gnorm,
      gws, gbs_rows, wbop, wbol, wbog, wout, spool, sconv, sh0, *acc)


def kernel(x_prompt, x_sample, c_prompt, c_sample, state_pool, state_conv, state_lru, w_ada, b_ada,
           norm_ffn1, ffn1_w_gate, ffn1_w_up, ffn1_w_down, norm_mix, w_in, pool_w, pool_scale,
           conv_w, conv_b, lru_wr, lru_br, lru_wi, lru_bi, lru_lambda, gmlp_norm, gmlp_ws, gmlp_bs,
           wbo_pool, wbo_lru, wbo_gmlp, w_out, norm_ffn2, ffn2_w_gate, ffn2_w_up, ffn2_w_down,
           norm_final):
    nbp, tp, _ = x_prompt.shape
    nbs, ts, _ = x_sample.shape
    tm_ffn, nb_ffn, tm_mix, nb_mix = 512, 64, 512, 32

    mods_p, mods_s = _ada(c_prompt, c_sample, w_ada, b_ada)

    def row(v):
        return v.reshape(DEPTH, 1, v.shape[-1])

    bf = lambda w: w.astype(BF16)
    f1 = (row(norm_ffn1), ffn1_w_gate, ffn1_w_up, ffn1_w_down)
    f2 = (row(norm_ffn2), ffn2_w_gate, ffn2_w_up, ffn2_w_down)
    wri = bf(0.5 * jnp.concatenate([lru_wr, lru_wi], axis=-1))
    col_scale = jnp.where(jnp.arange(IN_W) >= GATE_COL0, 0.5, 1.0).astype(F32)
    mixw = (row(norm_mix), bf(w_in * col_scale), bf(pool_w), row(pool_scale), conv_w, row(conv_b),
            wri, row(lru_br), row(lru_bi), row(lru_lambda), row(gmlp_norm),
            bf(0.5 * wbo_pool), bf(0.5 * wbo_lru), bf(0.5 * wbo_gmlp), bf(w_out))
    gbs_rows = jnp.repeat(jnp.swapaxes(gmlp_bs, 1, 2), GMLP_GW, axis=2)
    tperm = np.array([(r % SUBLANES) * SEG + r // SUBLANES for r in range(CHUNK)])
    gws_p = gmlp_ws[:, :, tperm][:, :, :, tperm]
    gbs_rows_p = gbs_rows[:, tperm, :]
    nf = norm_final.reshape(1, D)
    sh0 = state_lru.reshape(DEPTH, nbs, 1, D)

    xp = x_prompt.reshape(nbp, tp // CHUNK, SUBLANES, SEG, D).swapaxes(2, 3).reshape(nbp, tp, D)
    xs = x_sample
    pp, pc, ph = [], [], []
    acc = [jnp.zeros((DEPTH, nbs, POOL_PREFIX, D), F32), jnp.zeros((DEPTH, nbs, CONV_W - 1, D), F32),
           jnp.zeros((DEPTH, nbs, 1, D), F32), jnp.zeros((DEPTH, nbs, ts, GMLP_W), F32)]
    for l in range(DEPTH):
        final = l == DEPTH - 1
        xp, xs = _ffn(xp, xs, mods_p, mods_s, l, 0, *f1, nf, tm=tm_ffn, nb=nb_ffn, final=False)
        xp, tailp, tailc, nh = _mix_prompt(xp, mods_p, l, mixw, gws_p, gbs_rows_p, tm=tm_mix)
        npool = tailp[:, SEG - POOL_PREFIX:]
        nconv = tailc[:, SEG - (CONV_W - 1):]
        pp.append(npool); pc.append(nconv); ph.append(nh.reshape(nbp, D))
        xs, *acc = _mix_sample(xs, mods_s, l, mixw, gmlp_ws, gbs_rows,
                               state_pool, state_conv, sh0, acc, nb=nb_mix)
        xp, xs = _ffn(xp, xs, mods_p, mods_s, l, 6, *f2, nf, tm=tm_ffn, nb=nb_ffn, final=final)
    xp = xp.reshape(nbp, tp // CHUNK, SEG, SUBLANES, D).swapaxes(2, 3).reshape(nbp, tp, D)
    sp, sc, sh, sv = acc
    return (xp, xs, jnp.stack(pp), jnp.stack(pc), jnp.stack(ph),
            sp, sc, sh.reshape(DEPTH, nbs, D), sv)
```

```python
import functools

import numpy as np
import jax
import jax.numpy as jnp
from jax import lax
from jax.experimental import pallas as pl
from jax.experimental.pallas import tpu as pltpu

D = 1024
DEPTH = 4
PAST_LEN = 16384
POOL_WINDOWS = (2, 4, 8, 16)
POOL_GW = 256
POOL_PREFIX = 15
LRU_HEADS = 8
LRU_HD = 128
CONV_W = 4
LRU_C = 8.0
GMLP_W = 512
GMLP_GW = 128
CHUNK = 128
IN_W = 7168
D_FF = 2816
N_MOD = 9
EPS = 1e-6

GATE_COL0 = 3 * D + 2 * GMLP_W
LOG2E = 1.4426950408889634

FF_CHUNK = 256
N_FF_CHUNKS = D_FF // FF_CHUNK
NEXT_INPUT_PIECES = 8
SUBLANES = 8
LANES = 128
SEG = CHUNK // SUBLANES
CPREF = 8
VMEM_LIMIT = 56 * 1024 * 1024

F32 = jnp.float32
BF16 = jnp.bfloat16


def _dot(a, b):
    return jnp.dot(a, b, preferred_element_type=F32)


def _norm_mod(x, n, sc, sh):
    ms = jnp.mean(x * x, axis=-1, keepdims=True)
    y = x * lax.rsqrt(ms + EPS) * n
    return y * (1.0 + sc) + sh


def _ada_body(cp_ref, cs_ref, w_ref, b_ref, op_ref, os_ref):
    w = w_ref[...].astype(BF16)
    b = b_ref[...]

    def f(c):
        s = c * jax.nn.sigmoid(c)
        return _dot(s.astype(BF16), w) + b

    op_ref[...] = f(cp_ref[...])[:, None, :]
    os_ref[...] = f(cs_ref[...])[:, None, :]


def _ada(c_prompt, c_sample, w_ada, b_ada):
    nbp, nbs = c_prompt.shape[0], c_sample.shape[0]
    return pl.pallas_call(
        _ada_body,
        grid=(DEPTH, N_MOD),
        in_specs=[
            pl.BlockSpec((nbp, D), lambda l, j: (0, 0)),
            pl.BlockSpec((nbs, D), lambda l, j: (0, 0)),
            pl.BlockSpec((None, D, D), lambda l, j: (l, 0, j)),
            pl.BlockSpec((None, 1, D), lambda l, j: (l, 0, j)),
        ],
        out_specs=[
            pl.BlockSpec((None, None, nbp, 1, D), lambda l, j: (l, j, 0, 0, 0)),
            pl.BlockSpec((None, None, nbs, 1, D), lambda l, j: (l, j, 0, 0, 0)),
        ],
        out_shape=[
            jax.ShapeDtypeStruct((DEPTH, N_MOD, nbp, 1, D), F32),
            jax.ShapeDtypeStruct((DEPTH, N_MOD, nbs, 1, D), F32),
        ],
        name="ada",
    )(c_prompt, c_sample, w_ada, b_ada.reshape(DEPTH, 1, N_MOD * D))


def _ffn_input(x_ref, sh_ref, sc_ref, n_ref, rows=slice(None)):
    x = x_ref[:, rows, :]
    h = _norm_mod(x, n_ref[...], sc_ref[...], sh_ref[...])
    return h.reshape(x.shape[0] * x.shape[1], D).astype(BF16)


def _ffn_chunk(hb, wg, wu, wd):
    g = _dot(hb, wg)
    u = _dot(hb, wu)
    a = (g * jax.nn.sigmoid(g)) * u
    return _dot(a.astype(BF16), wd)


def _ffn_finish(x_ref, acc, gt_ref, nf_ref, o_ref, final):
    x = x_ref[...]
    out = x + (0.5 * gt_ref[...]) * acc.reshape(x.shape)
    if final:
        ms = jnp.mean(out * out, axis=-1, keepdims=True)
        out = out * lax.rsqrt(ms + EPS) * nf_ref[...]
    o_ref[...] = out


def _ffn_tile(x_ref, hb_get, gt_ref, wg_s, wu_s, wd_s, nf_ref, o_ref, final, chunk_hook=None):
    rows = x_ref.shape[0] * x_ref.shape[1]
    acc = jnp.zeros((rows, D), F32)
    for c in range(N_FF_CHUNKS):
        if chunk_hook is not None:
            chunk_hook(c)
        acc = acc + _ffn_chunk(hb_get(), wg_s[c], wu_s[c], wd_s[c])
    _ffn_finish(x_ref, acc, gt_ref, nf_ref, o_ref, final)


def _ffn_body(xp_ref, shp_ref, scp_ref, gtp_ref, xn_ref, shn_ref, scn_ref,
              xs_ref, shs_ref, scs_ref, gts_ref, n_ref,
              wg_ref, wu_ref, wd_ref, nf_ref, op_ref, os_ref, wg_s, wu_s, wd_s, hb_s, acc_s,
              *, n_p, final):
    i = pl.program_id(0)
    first_slot = 1 - N_FF_CHUNKS % 2

    @pl.when(i == 0)
    def _():
        hb_s[first_slot] = _ffn_input(xp_ref, shp_ref, scp_ref, n_ref)
        acc_s[...] = jnp.zeros(acc_s.shape, F32)

    @pl.when(i < N_FF_CHUNKS)
    def _():
        wg = wg_ref[...].astype(BF16)
        wu = wu_ref[...].astype(BF16)
        wd = wd_ref[...].astype(BF16)
        wg_s[i] = wg
        wu_s[i] = wu
        wd_s[i] = wd
        acc_s[...] += _ffn_chunk(hb_s[first_slot], wg, wu, wd)

    @pl.when(i == N_FF_CHUNKS - 1)
    def _():
        _ffn_finish(xp_ref, acc_s[...], gtp_ref, nf_ref, op_ref, final)
        hb_s[1 - first_slot] = _ffn_input(xn_ref, shn_ref, scn_ref, n_ref)

    @pl.when(jnp.logical_and(i >= N_FF_CHUNKS, i < N_FF_CHUNKS + n_p - 1))
    def _():
        slot = lax.rem(i, 2)

        tm = xp_ref.shape[1]
        piece = tm // NEXT_INPUT_PIECES

        def next_input(c):
            if c < NEXT_INPUT_PIECES:
                rows = slice(c * piece, (c + 1) * piece)
                hb_s[1 - slot, rows, :] = _ffn_input(xn_ref, shn_ref, scn_ref, n_ref, rows)

        _ffn_tile(xp_ref, lambda: hb_s[slot], gtp_ref, wg_s, wu_s, wd_s, nf_ref, op_ref, final,
                  chunk_hook=next_input)

    @pl.when(i >= N_FF_CHUNKS + n_p - 1)
    def _():
        hb = _ffn_input(xs_ref, shs_ref, scs_ref, n_ref)
        _ffn_tile(xs_ref, lambda: hb, gts_ref, wg_s, wu_s, wd_s, nf_ref, os_ref, final)


def _ffn(xp, xs, mods_p, mods_s, l, k0, norm, wg, wu, wd, norm_final, *, tm, nb, final):
    nbp, tp, _ = xp.shape
    nbs, ts, _ = xs.shape
    tpb = tp // tm
    n_p, n_s = nbp * tpb, nbs // nb

    def p_tile(i):
        return jnp.clip(i - (N_FF_CHUNKS - 1), 0, n_p - 1)

    def s_tile(i):
        return jnp.clip(i - (N_FF_CHUNKS + n_p - 1), 0, n_s - 1)

    def w_chunk(i):
        return jnp.minimum(i, N_FF_CHUNKS - 1)

    def modp(k, ahead=0):
        return pl.BlockSpec((None, None, 1, 1, D),
                            lambda i: (l, k, p_tile(i + ahead) // tpb, 0, 0))

    def mods(k):
        return pl.BlockSpec((None, None, nb, 1, D), lambda i: (l, k, s_tile(i), 0, 0))

    xp_spec = pl.BlockSpec((1, tm, D), lambda i: (p_tile(i) // tpb, p_tile(i) % tpb, 0))
    xn_spec = pl.BlockSpec((1, tm, D), lambda i: (p_tile(i + 1) // tpb, p_tile(i + 1) % tpb, 0))
    xs_spec = pl.BlockSpec((nb, ts, D), lambda i: (s_tile(i), 0, 0))
    w_col = pl.BlockSpec((None, D, FF_CHUNK), lambda i: (l, 0, w_chunk(i)))
    w_row = pl.BlockSpec((None, FF_CHUNK, D), lambda i: (l, w_chunk(i), 0))
    return pl.pallas_call(
        functools.partial(_ffn_body, n_p=n_p, final=final),
        grid=(N_FF_CHUNKS + n_p - 1 + n_s,),
        in_specs=[
            xp_spec, modp(k0), modp(k0 + 1), modp(k0 + 2),
            xn_spec, modp(k0, 1), modp(k0 + 1, 1),
            xs_spec, mods(k0), mods(k0 + 1), mods(k0 + 2),
            pl.BlockSpec((None, 1, D), lambda i: (l, 0, 0)),
            w_col, w_col, w_row,
            pl.BlockSpec((1, D), lambda i: (0, 0)),
        ],
        out_specs=[xp_spec, xs_spec],
        out_shape=[jax.ShapeDtypeStruct(xp.shape, F32), jax.ShapeDtypeStruct(xs.shape, F32)],
        scratch_shapes=[
            pltpu.VMEM((N_FF_CHUNKS, D, FF_CHUNK), BF16),
            pltpu.VMEM((N_FF_CHUNKS, D, FF_CHUNK), BF16),
            pltpu.VMEM((N_FF_CHUNKS, FF_CHUNK, D), BF16),
            pltpu.VMEM((2, tm, D), BF16),
            pltpu.VMEM((tm, D), F32),
        ],
        compiler_params=pltpu.CompilerParams(
            dimension_semantics=("arbitrary",), vmem_limit_bytes=VMEM_LIMIT),
        name="ffn_final" if final else "ffn",
    )(xp, mods_p, mods_p, mods_p, xp, mods_p, mods_p, xs, mods_s, mods_s, mods_s,
      norm, wg, wu, wd, norm_final)


def _mod_spec(l, k, nb):
    return pl.BlockSpec((None, None, nb, 1, D), lambda i, j: (l, k, i, 0, 0))


def _row_spec(l, width):
    return pl.BlockSpec((None, 1, width), lambda i, j: (l, 0, 0))


def _full_spec(l, shape):
    nd = len(shape)
    return pl.BlockSpec((None,) + tuple(shape), lambda i, j: (l,) + (0,) * nd,
                        pipeline_mode=pl.Buffered(1))


def _scan8(a, b):
    row = lax.broadcasted_iota(jnp.int32, a.shape, 0) & (SUBLANES - 1)
    for d in (1, 2, 4):
        m = row >= d
        a_s = jnp.where(m, pltpu.roll(a, d, 0), 1.0)
        b_s = jnp.where(m, pltpu.roll(b, d, 0), 0.0)
        b = a * b_s + b
        a = a * a_s
    return a, b


def _lru_coeffs(xc, wri_ref, br, bi, lam, head0=0):
    xcb = xc.astype(BF16)
    rs, is_ = [], []
    for hd in range(xc.shape[1] // LRU_HD):
        res = _dot(xcb[:, hd * LRU_HD:(hd + 1) * LRU_HD], wri_ref[head0 + hd])
        rs.append(res[:, :LRU_HD])
        is_.append(res[:, LRU_HD:])
    tr = jnp.tanh(jnp.concatenate(rs, axis=1) + 0.5 * br)
    ti = jnp.tanh(jnp.concatenate(is_, axis=1) + 0.5 * bi)
    nl = -lam
    softplus = jnp.maximum(nl, 0.0) + jnp.log1p(jnp.exp(-jnp.abs(nl)))
    c1 = (-0.5 * LRU_C * LOG2E) * softplus
    a = jnp.exp2(c1 * (tr + 1.0))
    y = 1.0 - a * a
    root = jnp.where(y > 0.0, y * lax.rsqrt(y), 0.0)
    bx = root * ((0.5 * ti + 0.5) * xc)
    return a, bx


def _pool_project(d_parts, poolw_ref, pscale):
    ys = [_dot(d.astype(BF16), poolw_ref[g]) for g, d in enumerate(d_parts)]
    return jnp.concatenate(ys, axis=1) * pscale


def _rms(v, g):
    ms = jnp.mean(v * v, axis=-1, keepdims=True)
    return v * lax.rsqrt(ms + EPS) * g


def _merge_out(hb_get, win_ref, t_pool, t_lru, y_gm, wbog_ref, wout_ref):
    def gate(k):
        c0 = GATE_COL0 + k * D
        return jnp.tanh(_dot(hb_get(), win_ref[:, c0:c0 + D])) + 1.0

    merged = gate(0) * t_pool
    merged = merged + gate(1) * t_lru
    merged = merged + gate(2) * _dot(y_gm.astype(BF16), wbog_ref[...])
    return _dot(merged.astype(BF16), wout_ref[...])


def _extend(x, carry_ref, e_ref, cols):
    tm, cdim = x.shape
    nc = tm // CHUNK
    xr = pltpu.roll(x.reshape(tm // SUBLANES, SUBLANES, cdim), 1, 1).reshape(tm, cdim)
    prev = carry_ref[:, cols]
    if nc > 1:
        prev = jnp.concatenate([prev, xr[:tm - CHUNK]], axis=0)
    sub0 = (lax.broadcasted_iota(jnp.int32, (tm, cdim), 0) & (SUBLANES - 1)) == 0
    e_ref[:, 0:CHUNK, cols] = jnp.where(sub0, prev, xr).reshape(nc, CHUNK, cdim)
    e_ref[:, CHUNK:2 * CHUNK, cols] = x.reshape(nc, CHUNK, cdim)
    carry_ref[:, cols] = xr[tm - CHUNK:]


def _shifted(e_ref, k, cols=slice(None)):
    v = e_ref[:, pl.ds((SEG - k) * SUBLANES, CHUNK), cols]
    return v.reshape(v.shape[0] * CHUNK, v.shape[2])


def _window_sum(e_ref, ta_ref, tb_ref, c0, w):
    src, cols = e_ref, slice(c0, c0 + POOL_GW)
    bufs = (ta_ref, tb_ref)
    k, level, lo = 1, 0, 0
    while True:
        last = 2 * k == w
        lo = lo + k
        start = SEG if last else lo
        n = (2 * SEG - start) * SUBLANES
        val = (src[:, pl.ds(start * SUBLANES, n), cols]
               + src[:, pl.ds((start - k) * SUBLANES, n), cols])
        if last:
            return val.reshape(val.shape[0] * CHUNK, POOL_GW)
        dst = bufs[level % 2]
        dst[:, pl.ds(start * SUBLANES, n), :] = val
        src, cols = dst, slice(0, POOL_GW)
        k, level = 2 * k, level + 1


def _stored_time(shape, axis):
    r = lax.broadcasted_iota(jnp.int32, shape, axis)
    return (r & (SUBLANES - 1)) * SEG + ((r >> 3) & (SEG - 1))


def _mix_prompt_body(x_ref, sh_ref, sc_ref, gt_ref, xn_ref, shn_ref, scn_ref,
                     nm_ref, win_ref, poolw_ref, pscale_ref,
                     convw_ref, convb_ref, wri_ref, br_ref, bi_ref, lam_ref, gnorm_ref, gws_ref,
                     gbs_ref, wbop_ref, wbol_ref, wbog_ref, wout_ref,
                     _tailp_acc, _tailc_acc, _h_acc,
                     o_ref, tailp_ref, tailc_ref, nh_ref,
                     ep_ref, cp_ref, ta_ref, tb_ref, ec_ref, cc_ref, pa_ref, h_ref, hc_ref,
                     yg_ref, hb_s, *, tm):
    j = pl.program_id(1)
    nt = pl.num_programs(1)
    nc = tm // CHUNK
    step = pl.program_id(0) * nt + j
    slot = lax.rem(step, 2)

    @pl.when(j == 0)
    def _():
        cp_ref[...] = jnp.zeros((CHUNK, D), F32)
        cc_ref[...] = jnp.zeros((CHUNK, D), F32)
        hc_ref[...] = jnp.zeros((1, D), F32)

    @pl.when(step == 0)
    def _():
        hb_s[0] = _ffn_input(x_ref, sh_ref, sc_ref, nm_ref)

    x = x_ref[...]
    piece = tm // NEXT_INPUT_PIECES

    def hb_get():
        return hb_s[slot]

    def next_input(k):
        rows = slice(k * piece, (k + 1) * piece)
        hb_s[1 - slot, rows, :] = _ffn_input(xn_ref, shn_ref, scn_ref, nm_ref, rows)

    all_cols = slice(0, D)
    xp = _dot(hb_get(), win_ref[:, 0:D])
    next_input(0)
    _extend(xp, cp_ref, ep_ref, all_cols)
    row = lax.broadcasted_iota(jnp.int32, (tm, LANES), 0)
    pos = j * tm + (row & ~(CHUNK - 1)) + _stored_time((tm, LANES), 0)
    d_parts = []
    for g, w in enumerate(POOL_WINDOWS):
        s = _window_sum(ep_ref, ta_ref, tb_ref, g * POOL_GW, w)
        inv = 1.0 / jnp.minimum(w, pos + 1).astype(F32)
        inv = jnp.concatenate([inv] * (POOL_GW // LANES), axis=1)
        d_parts.append(s * inv - xp[:, g * POOL_GW:(g + 1) * POOL_GW])
    y_pool = _pool_project(d_parts, poolw_ref, pscale_ref[...])
    next_input(1)

    xl = _dot(hb_get(), win_ref[:, D:2 * D])
    next_input(2)
    _extend(xl, cc_ref, ec_ref, all_cols)
    cw = convw_ref[...]
    xc = convb_ref[...] + cw[CONV_W - 1:CONV_W, :] * xl
    for k in range(CONV_W - 1):
        xc = xc + cw[k:k + 1, :] * _shifted(ec_ref, CONV_W - 1 - k)

    tailp_ref[...] = xp[tm - CHUNK:, :].reshape(SEG, SUBLANES, D)[:, SUBLANES - 1, :]
    tailc_ref[...] = xl[tm - CHUNK:, :].reshape(SEG, SUBLANES, D)[:, SUBLANES - 1, :]

    a, bx = _lru_coeffs(xc, wri_ref, br_ref[...], bi_ref[...], lam_ref[...])
    next_input(3)
    a3 = a.reshape(nc, CHUNK, D)
    b3 = bx.reshape(nc, CHUNK, D)
    hq = b3[:, 0:SUBLANES, :]
    paq = a3[:, 0:SUBLANES, :]
    pa_ref[:, 0:SUBLANES, :] = paq
    h_ref[:, 0:SUBLANES, :] = hq
    for q in range(1, SEG):
        rows = slice(q * SUBLANES, (q + 1) * SUBLANES)
        hq = a3[:, rows, :] * hq + b3[:, rows, :]
        paq = a3[:, rows, :] * paq
        pa_ref[:, rows, :] = paq
        h_ref[:, rows, :] = hq
    g_state = hc_ref[...]
    entry = []
    for c in range(nc):
        rows_c = []
        for s in range(SUBLANES):
            rows_c.append(g_state)
            g_state = paq[c, s:s + 1, :] * g_state + hq[c, s:s + 1, :]
        entry.append(jnp.concatenate(rows_c, axis=0))
    entry = jnp.stack(entry, axis=0)
    hc_ref[...] = g_state
    nh_ref[...] = g_state

    hs = [pa_ref[:, q * SUBLANES:(q + 1) * SUBLANES, :] * entry
          + h_ref[:, q * SUBLANES:(q + 1) * SUBLANES, :] for q in range(SEG)]
    hs = jnp.concatenate(hs, axis=1).reshape(tm, D)
    next_input(4)
    gl = _dot(hb_get(), win_ref[:, 2 * D:3 * D])
    y_lru = hs * jax.nn.gelu(gl)
    t_pool = _dot(y_pool.astype(BF16), wbop_ref[...])
    t_lru = _dot(y_lru.astype(BF16), wbol_ref[...])

    next_input(5)
    u = _dot(hb_get(), win_ref[:, 3 * D:3 * D + GMLP_W])
    next_input(6)
    v = _dot(hb_get(), win_ref[:, 3 * D + GMLP_W:4 * D])
    vnb = _rms(v, gnorm_ref[...]).astype(BF16)
    tri = _stored_time((CHUNK, CHUNK), 0) >= _stored_time((CHUNK, CHUNK), 1)
    gbs = gbs_ref[...]
    for g in range(GMLP_W // GMLP_GW):
        wm = jnp.where(tri, gws_ref[g], 0.0).astype(BF16)
        cols = slice(g * GMLP_GW, (g + 1) * GMLP_GW)
        for c in range(nc):
            rows = slice(c * CHUNK, (c + 1) * CHUNK)
            s = _dot(wm, vnb[rows, cols]) + gbs[:, cols]
            yg_ref[rows, cols] = u[rows, cols] * s
    y_gm = yg_ref[...]
    next_input(7)

    y = _merge_out(hb_get, win_ref, t_pool, t_lru, y_gm, wbog_ref, wout_ref)
    o_ref[...] = x + gt_ref[...] * y.reshape(1, tm, D)


def _mix_weight_specs(l):
    return [
        _row_spec(l, D),
        _full_spec(l, (D, IN_W)),
        _full_spec(l, (len(POOL_WINDOWS), POOL_GW, POOL_GW)),
        _row_spec(l, D),
        pl.BlockSpec((None, CONV_W, D), lambda i, j: (l, 0, 0)),
        _row_spec(l, D),
        _full_spec(l, (LRU_HEADS, LRU_HD, 2 * LRU_HD)),
        _row_spec(l, D), _row_spec(l, D), _row_spec(l, D),
        _row_spec(l, GMLP_W),
    ]


def _mix_proj_specs(l):
    return [_full_spec(l, (D, D)), _full_spec(l, (D, D)), _full_spec(l, (GMLP_W, D)),
            _full_spec(l, (D, D))]


def _mix_prompt(x, mods, l, wts, gws, gbs_rows, acc, *, tm):
    nbt, tt, _ = x.shape
    grid = (nbt, tt // tm)
    nt = tt // tm
    last = nbt * nt - 1
    xspec = pl.BlockSpec((1, tm, D), lambda i, j: (i, j, 0))

    def nxt(i, j):
        s2 = jnp.minimum(i * nt + j + 1, last)
        return s2 // nt, s2 % nt

    xn_spec = pl.BlockSpec((1, tm, D), lambda i, j: (*nxt(i, j), 0))

    def modn(k):
        return pl.BlockSpec((None, None, 1, 1, D), lambda i, j: (l, k, nxt(i, j)[0], 0, 0))

    in_specs = ([xspec, _mod_spec(l, 3, 1), _mod_spec(l, 4, 1), _mod_spec(l, 5, 1),
                 xn_spec, modn(3), modn(4)]
                + _mix_weight_specs(l)
                + [pl.BlockSpec((None, GMLP_W // GMLP_GW, CHUNK, CHUNK), lambda i, j: (l, 0, 0, 0)),
                   pl.BlockSpec((None, CHUNK, GMLP_W), lambda i, j: (l, 0, 0))]
                + _mix_proj_specs(l))
    n_in = len(in_specs)
    in_specs = in_specs + [pl.BlockSpec(memory_space=pl.ANY)] * len(acc)
    nc = tm // CHUNK
    out_specs = [
        xspec,
        pl.BlockSpec((None, None, SEG, D), lambda i, j: (l, i, 0, 0)),
        pl.BlockSpec((None, None, SEG, D), lambda i, j: (l, i, 0, 0)),
        pl.BlockSpec((None, None, 1, D), lambda i, j: (l, i, 0, 0)),
    ]
    out_shape = [jax.ShapeDtypeStruct(x.shape, F32)] + [
        jax.ShapeDtypeStruct(a.shape, F32) for a in acc]
    scratch = [
        pltpu.VMEM((nc, 2 * CHUNK, D), F32),
        pltpu.VMEM((CHUNK, D), F32),
        pltpu.VMEM((nc, 2 * CHUNK, POOL_GW), F32),
        pltpu.VMEM((nc, 2 * CHUNK, POOL_GW), F32),
        pltpu.VMEM((nc, 2 * CHUNK, D), F32),
        pltpu.VMEM((CHUNK, D), F32),
        pltpu.VMEM((nc, CHUNK, D), F32),
        pltpu.VMEM((nc, CHUNK, D), F32),
        pltpu.VMEM((1, D), F32),
        pltpu.VMEM((tm, GMLP_W), F32),
        pltpu.VMEM((2, tm, D), BF16),
    ]
    (norm, w_in, pool_w, pool_scale, conv_w, conv_b, wri, br, bi, lam, gnorm,
     wbop, wbol, wbog, wout) = wts
    return pl.pallas_call(
        functools.partial(_mix_prompt_body, tm=tm),
        grid=grid,
        in_specs=in_specs,
        out_specs=out_specs,
        out_shape=out_shape,
        scratch_shapes=scratch,
        input_output_aliases={n_in + k: 1 + k for k in range(len(acc))},
        compiler_params=pltpu.CompilerParams(
            dimension_semantics=("arbitrary", "arbitrary"), vmem_limit_bytes=VMEM_LIMIT),
        name="mix_prompt",
    )(x, mods, mods, mods, x, mods, mods, norm, w_in, pool_w, pool_scale, conv_w, conv_b, wri, br, bi, lam, gnorm,
      gws, gbs_rows, wbop, wbol, wbog, wout, *acc)


def _mix_sample_body(x_ref, sh_ref, sc_ref, gt_ref, nm_ref, win_ref, poolw_ref, pscale_ref,
                     convw_ref, convb_ref, wri_ref, br_ref, bi_ref, lam_ref, gnorm_ref, gws_ref,
                     gbs_ref, wbop_ref, wbol_ref, wbog_ref, wout_ref,
                     spool_ref, sconv_ref, sh0_ref, _pool_acc, _conv_acc, _h_acc, _vn_acc,
                     o_ref, npool_ref, nconv_ref, nh_ref, vn_ref,
                     p_ref, c_ref, *, nb, t):
    rows = nb * t
    x = x_ref[...]
    h = _norm_mod(x, nm_ref[...], sc_ref[...], sh_ref[...])
    hb = h.reshape(rows, D).astype(BF16)

    xp = _dot(hb, win_ref[:, 0:D]).reshape(nb, t, D)
    p_ref[:, pl.ds(1, POOL_PREFIX), :] = spool_ref[...]
    p_ref[:, pl.ds(POOL_PREFIX + 1, t), :] = xp
    d_parts = []
    for g, w in enumerate(POOL_WINDOWS):
        cols = slice(g * POOL_GW, (g + 1) * POOL_GW)
        s = xp[:, :, cols]
        for k in range(1, w):
            s = s + p_ref[:, pl.ds(POOL_PREFIX + 1 - k, t), cols]
        d_parts.append((s / float(w) - xp[:, :, cols]).reshape(rows, POOL_GW))
    y_pool = _pool_project(d_parts, poolw_ref, pscale_ref[...])
    npool_ref[...] = p_ref[:, pl.ds(t + 1, POOL_PREFIX), :]

    xl = _dot(hb, win_ref[:, D:2 * D]).reshape(nb, t, D)
    c_ref[:, pl.ds(CPREF - (CONV_W - 1), CONV_W - 1), :] = sconv_ref[...]
    c_ref[:, pl.ds(CPREF, t), :] = xl
    cw = convw_ref[...]
    xc = convb_ref[...] + cw[3:4, :] * xl
    for k in range(CONV_W - 1):
        xc = xc + cw[k:k + 1, :] * c_ref[:, pl.ds(CPREF - (CONV_W - 1) + k, t), :]
    nconv_ref[...] = c_ref[:, pl.ds(CPREF + t - (CONV_W - 1), CONV_W - 1), :]

    a, bx = _lru_coeffs(xc.reshape(rows, D), wri_ref, br_ref[...], bi_ref[...], lam_ref[...])
    a, bx = _scan8(a, bx)
    hs = a.reshape(nb, t, D) * sh0_ref[...] + bx.reshape(nb, t, D)
    nh_ref[...] = hs[:, t - 1:t, :]
    gl = _dot(hb, win_ref[:, 2 * D:3 * D])
    y_lru = hs.reshape(rows, D) * jax.nn.gelu(gl)

    u = _dot(hb, win_ref[:, 3 * D:3 * D + GMLP_W])
    v = _dot(hb, win_ref[:, 3 * D + GMLP_W:4 * D])
    vn = _rms(v, gnorm_ref[...]).reshape(nb, t, GMLP_W)
    vn_ref[...] = vn
    tri = (lax.broadcasted_iota(jnp.int32, (t, t), 0) >= lax.broadcasted_iota(jnp.int32, (t, t), 1))
    gbs = gbs_ref[...]
    s_parts = []
    for g in range(GMLP_W // GMLP_GW):
        cols = slice(g * GMLP_GW, (g + 1) * GMLP_GW)
        wm = jnp.where(tri, gws_ref[g][0:t, 0:t], 0.0)
        s = jnp.broadcast_to(gbs[0:t, cols], (nb, t, GMLP_GW))
        for k in range(t):
            s = s + wm[:, k:k + 1] * vn[:, k:k + 1, cols]
        s_parts.append(s)
    y_gm = u * jnp.concatenate(s_parts, axis=2).reshape(rows, GMLP_W)

    t_pool = _dot(y_pool.astype(BF16), wbop_ref[...])
    t_lru = _dot(y_lru.astype(BF16), wbol_ref[...])
    y = _merge_out(lambda: hb, win_ref, t_pool, t_lru, y_gm, wbog_ref, wout_ref)
    o_ref[...] = x + gt_ref[...] * y.reshape(nb, t, D)


def _mix_sample(x, mods, l, wts, gws, gbs_rows, spool, sconv, sh0, acc, *, nb):
    nbt, t, _ = x.shape
    grid = (nbt // nb, 1)
    xspec = pl.BlockSpec((nb, t, D), lambda i, j: (i, 0, 0))

    def state_spec(r):
        return pl.BlockSpec((None, nb, r, D), lambda i, j: (l, i, 0, 0))

    def out_state_spec(r, width=D):
        return pl.BlockSpec((None, nb, r, width), lambda i, j: (l, i, 0, 0))

    n_in = 24

    in_specs = ([xspec, _mod_spec(l, 3, nb), _mod_spec(l, 4, nb), _mod_spec(l, 5, nb)]
                + _mix_weight_specs(l)
                + [pl.BlockSpec((None, GMLP_W // GMLP_GW, CHUNK, CHUNK), lambda i, j: (l, 0, 0, 0)),
                   pl.BlockSpec((None, CHUNK, GMLP_W), lambda i, j: (l, 0, 0))]
                + _mix_proj_specs(l)
                + [state_spec(POOL_PREFIX), state_spec(CONV_W - 1), state_spec(1)]
                + [pl.BlockSpec(memory_space=pl.ANY)] * len(acc))
    assert len(in_specs) == n_in + len(acc)
    out_specs = [xspec, out_state_spec(POOL_PREFIX), out_state_spec(CONV_W - 1), out_state_spec(1),
                 out_state_spec(t, GMLP_W)]
    out_shape = [jax.ShapeDtypeStruct(x.shape, F32)] + [
        jax.ShapeDtypeStruct(a.shape, F32) for a in acc]
    scratch = [
        pltpu.VMEM((nb, POOL_PREFIX + 1 + t, D), F32),
        pltpu.VMEM((nb, CPREF + t, D), F32),
    ]
    (norm, w_in, pool_w, pool_scale, conv_w, conv_b, wri, br, bi, lam, gnorm,
     wbop, wbol, wbog, wout) = wts
    return pl.pallas_call(
        functools.partial(_mix_sample_body, nb=nb, t=t),
        grid=grid,
        in_specs=in_specs,
        out_specs=out_specs,
        out_shape=out_shape,
        scratch_shapes=scratch,
        input_output_aliases={n_in + k: 1 + k for k in range(len(acc))},
        compiler_params=pltpu.CompilerParams(
            dimension_semantics=("arbitrary", "arbitrary"), vmem_limit_bytes=VMEM_LIMIT),
        name="mix_sample",
    )(x, mods, mods, mods, norm, w_in, pool_w, pool_scale, conv_w, conv_b, wri, br, bi, lam, gnorm,
      gws, gbs_rows, wbop, wbol, wbog, wout, spool, sconv, sh0, *acc)


def kernel(x_prompt, x_sample, c_prompt, c_sample, state_pool, state_conv, state_lru, w_ada, b_ada,
           norm_ffn1, ffn1_w_gate, ffn1_w_up, ffn1_w_down, norm_mix, w_in, pool_w, pool_scale,
           conv_w, conv_b, lru_wr, lru_br, lru_wi, lru_bi, lru_lambda, gmlp_norm, gmlp_ws, gmlp_bs,
           wbo_pool, wbo_lru, wbo_gmlp, w_out, norm_ffn2, ffn2_w_gate, ffn2_w_up, ffn2_w_down,
           norm_final):
    nbp, tp, _ = x_prompt.shape
    nbs, ts, _ = x_sample.shape
    tm_ffn, nb_ffn, tm_mix, nb_mix = 512, 64, 512, 32

    mods_p, mods_s = _ada(c_prompt, c_sample, w_ada, b_ada)

    def row(v):
        return v.reshape(DEPTH, 1, v.shape[-1])

    bf = lambda w: w.astype(BF16)
    f1 = (row(norm_ffn1), ffn1_w_gate, ffn1_w_up, ffn1_w_down)
    f2 = (row(norm_ffn2), ffn2_w_gate, ffn2_w_up, ffn2_w_down)
    wri = bf(0.5 * jnp.concatenate([lru_wr, lru_wi], axis=-1))
    col_scale = jnp.where(jnp.arange(IN_W) >= GATE_COL0, 0.5, 1.0).astype(F32)
    mixw = (row(norm_mix), bf(w_in * col_scale), bf(pool_w), row(pool_scale), conv_w, row(conv_b),
            wri, row(lru_br), row(lru_bi), row(lru_lambda), row(gmlp_norm),
            bf(0.5 * wbo_pool), bf(0.5 * wbo_lru), bf(0.5 * wbo_gmlp), bf(w_out))
    gbs_rows = jnp.repeat(jnp.swapaxes(gmlp_bs, 1, 2), GMLP_GW, axis=2)
    tperm = np.array([(r % SUBLANES) * SEG + r // SUBLANES for r in range(CHUNK)])
    gws_p = gmlp_ws[:, :, tperm][:, :, :, tperm]
    gbs_rows_p = gbs_rows[:, tperm, :]
    nf = norm_final.reshape(1, D)
    sh0 = state_lru.reshape(DEPTH, nbs, 1, D)

    xp = x_prompt.reshape(nbp, tp // CHUNK, SUBLANES, SEG, D).swapaxes(2, 3).reshape(nbp, tp, D)
    xs = x_sample
    accp = [jnp.zeros((DEPTH, nbp, SEG, D), F32), jnp.zeros((DEPTH, nbp, SEG, D), F32),
            jnp.zeros((DEPTH, nbp, 1, D), F32)]
    acc = [jnp.zeros((DEPTH, nbs, POOL_PREFIX, D), F32), jnp.zeros((DEPTH, nbs, CONV_W - 1, D), F32),
           jnp.zeros((DEPTH, nbs, 1, D), F32), jnp.zeros((DEPTH, nbs, ts, GMLP_W), F32)]
    for l in range(DEPTH):
        final = l == DEPTH - 1
        xp, xs = _ffn(xp, xs, mods_p, mods_s, l, 0, *f1, nf, tm=tm_ffn, nb=nb_ffn, final=False)
        xp, *accp = _mix_prompt(xp, mods_p, l, mixw, gws_p, gbs_rows_p, accp, tm=tm_mix)
        xs, *acc = _mix_sample(xs, mods_s, l, mixw, gmlp_ws, gbs_rows,
                               state_pool, state_conv, sh0, acc, nb=nb_mix)
        xp, xs = _ffn(xp, xs, mods_p, mods_s, l, 6, *f2, nf, tm=tm_ffn, nb=nb_ffn, final=final)
    xp = xp.reshape(nbp, tp // CHUNK, SEG, SUBLANES, D).swapaxes(2, 3).reshape(nbp, tp, D)
    sp, sc, sh, sv = acc
    tailp, tailc, ph = accp
    return (xp, xs, tailp[:, :, SEG - POOL_PREFIX:], tailc[:, :, SEG - (CONV_W - 1):],
            ph.reshape(DEPTH, nbp, D), sp, sc, sh.reshape(DEPTH, nbs, D), sv)
```
